```python
import math
import jax, jax.numpy as jnp
from jax import lax
import numpy as np

D_MODEL = 1024
BATCH = 16
SEQ = 4096
DEPTH = 1
DEC_BATCH = 32
DEC_SEQ = 16
PAST_LEN = 2048

CHUNK = 64
MIX_WIDTH = D_MODEL
RET_WIDTH = MIX_WIDTH // 2
RET_HEADS = 4
RET_HEAD_DIM = RET_WIDTH // RET_HEADS
SSM_WIDTH = MIX_WIDTH - RET_WIDTH
SSM_GROUP = 16
SSM_GROUPS = SSM_WIDTH // SSM_GROUP
SSM_STATE = 64
IN_WIDTH = 4 * RET_WIDTH + 2 * SSM_WIDTH
ROPE_BASE = 10000.0
EPS = 1e-6
DT_MIN = 1e-3
DT_MAX = 1e-1
LAMBDA_RE_MAX = -1e-4

kernel_name = "retnet_s5_parallel_heads_stream_step"


def _rmsnorm(x, g):
    xf = x.astype(jnp.float32)
    return xf * lax.rsqrt(jnp.mean(xf * xf, axis=-1, keepdims=True) + EPS) * g.astype(jnp.float32)


def _rotary(x, pos):
    half = x.shape[-1] // 2
    inv = ROPE_BASE ** (-jnp.arange(half, dtype=jnp.float32) / half)
    ang = pos.astype(jnp.float32)[:, None] * inv[None, :]
    cos = jnp.cos(ang)[None, :, None, :]
    sin = jnp.sin(ang)[None, :, None, :]
    x1, x2 = x[..., :half], x[..., half:]
    return jnp.concatenate([x1 * cos - x2 * sin, x1 * sin + x2 * cos], axis=-1)


def _retention(q, k, v, s0):
    b, t, h, dh = q.shape
    blk = min(t, CHUNK)
    n = t // blk
    log_g = jnp.log1p(-jnp.power(2.0, -5.0 - jnp.arange(RET_HEADS, dtype=jnp.float32)))
    idx = jnp.arange(blk, dtype=jnp.float32)
    diff = idx[:, None] - idx[None, :]
    causal = diff >= 0
    intra_decay = jnp.where(causal[None], jnp.exp(jnp.where(causal, diff, 0.0)[None] * log_g[:, None, None]), 0.0)
    q_decay = jnp.exp((idx + 1.0)[:, None] * log_g[None, :])
    k_decay = jnp.exp((blk - 1.0 - idx)[:, None] * log_g[None, :])
    s_decay = jnp.exp(blk * log_g)

    def step(s, qkv):
        qc, kc, vc = qkv
        scores = jnp.einsum('blhd,bmhd->bhlm', qc, kc) * intra_decay[None]
        o = (jnp.einsum('bhlm,bmhe->blhe', scores, vc)
             + jnp.einsum('blhd,bhde->blhe', qc, s) * q_decay[None, :, :, None])
        s_new = s_decay[None, :, None, None] * s + jnp.einsum('blhd,blhe->bhde', kc * k_decay[None, :, :, None], vc)
        return s_new, o

    split = lambda a: a.reshape(b, n, blk, h, dh).swapaxes(0, 1)
    s_fin, o = lax.scan(step, s0, (split(q), split(k), split(v)))
    return o.swapaxes(0, 1).reshape(b, t, h, dh), s_fin


def _s5(u, h0, lambda_re, lambda_im, log_step, b_re, b_im, c_re, c_im, d_skip):
    bsz, t, _ = u.shape
    f32 = jnp.float32
    lam = lax.complex(jnp.minimum(lambda_re.astype(f32), LAMBDA_RE_MAX), lambda_im.astype(f32))
    dt = jnp.exp(log_step.astype(f32))[:, None]
    lam_bar = jnp.exp(lam * dt)
    b_mat = lax.complex(b_re.astype(f32), b_im.astype(f32))
    b_bar = ((lam_bar - 1.0) / lam)[..., None] * b_mat
    c_mat = lax.complex(c_re.astype(f32), c_im.astype(f32))
    ug = u.reshape(bsz, t, SSM_GROUPS, SSM_GROUP).astype(jnp.complex64)
    bu = jnp.einsum('gpc,btgc->btgp', b_bar, ug)
    bu = bu.at[:, 0].add(lam_bar[None] * h0)
    a = jnp.broadcast_to(lam_bar, bu.shape)

    def combine(left, right):
        a1, b1 = left
        a2, b2 = right
        return a2 * a1, a2 * b1 + b2

    _, hs = lax.associative_scan(combine, (a, bu), axis=1)
    y = jnp.einsum('gcp,btgp->btgc', c_mat, hs).real.reshape(bsz, t, SSM_WIDTH)
    y = y + d_skip.astype(f32) * u
    return y, hs[:, -1]


def _layer(x, pos, ret_s0, ssm_h0, norm_g, w_in, ret_norm_g, lambda_re, lambda_im, log_step,
           b_re, b_im, c_re, c_im, d_skip, w_glu, b_glu, w_out):
    bsz, t, _ = x.shape
    f32 = jnp.float32
    hn = _rmsnorm(x, norm_g).astype(x.dtype)
    proj = (hn @ w_in).astype(f32)
    q, k, v, g_ret, u, g_ssm = jnp.split(
        proj, np.cumsum([RET_WIDTH, RET_WIDTH, RET_WIDTH, RET_WIDTH, SSM_WIDTH]).tolist(), axis=-1)
    heads = lambda a: a.reshape(bsz, t, RET_HEADS, RET_HEAD_DIM)
    q = _rotary(heads(q), pos)
    k = _rotary(heads(k), pos) * (RET_HEAD_DIM ** -0.5)
    o, ret_s = _retention(q, k, heads(v), ret_s0.astype(f32))
    o = _rmsnorm(o, ret_norm_g).reshape(bsz, t, RET_WIDTH) * jax.nn.silu(g_ret)
    y, ssm_h = _s5(u, ssm_h0, lambda_re, lambda_im, log_step, b_re, b_im, c_re, c_im, d_skip)
    z = jax.nn.gelu(y)
    z = z * jax.nn.sigmoid(z @ w_glu.astype(f32) + b_glu.astype(f32))
    z = z * jax.nn.silu(g_ssm)
    mix = jnp.concatenate([o, z], axis=-1).astype(x.dtype)
    return x + mix @ w_out, ret_s, ssm_h


def setup_inputs(seed: int = 0) -> dict:
    key = jax.random.key(seed)
    ks = jax.random.split(key, 20)
    f32 = jnp.float32
    nrm = lambda k, shp, s: jax.random.normal(k, shp, f32) * s
    n_idx = jnp.arange(SSM_STATE, dtype=f32)
    return {
        "x_prompt": nrm(ks[0], (BATCH, SEQ, D_MODEL), 1.0),
        "x_sample": nrm(ks[1], (DEC_BATCH, DEC_SEQ, D_MODEL), 1.0),
        "state_ret": nrm(ks[2], (DEPTH, DEC_BATCH, RET_HEADS, RET_HEAD_DIM, RET_HEAD_DIM), 0.3),
        "state_ssm_re": nrm(ks[3], (DEPTH, DEC_BATCH, SSM_GROUPS, SSM_STATE), 0.1),
        "state_ssm_im": nrm(ks[4], (DEPTH, DEC_BATCH, SSM_GROUPS, SSM_STATE), 0.1),
        "norm_g": 1.0 + nrm(ks[5], (DEPTH, D_MODEL), 0.01),
        "w_in": nrm(ks[6], (DEPTH, D_MODEL, IN_WIDTH), D_MODEL ** -0.5),
        "ret_norm_g": 1.0 + nrm(ks[7], (DEPTH, RET_HEADS, RET_HEAD_DIM), 0.01),
        "ssm_lambda_re": -0.5 + nrm(ks[8], (DEPTH, SSM_GROUPS, SSM_STATE), 0.01),
        "ssm_lambda_im": math.pi * n_idx + nrm(ks[9], (DEPTH, SSM_GROUPS, SSM_STATE), 0.01),
        "ssm_log_step": jax.random.uniform(ks[10], (DEPTH, SSM_GROUPS), f32, math.log(DT_MIN), math.log(DT_MAX)),
        "ssm_b_re": nrm(ks[11], (DEPTH, SSM_GROUPS, SSM_STATE, SSM_GROUP), (2.0 * SSM_GROUP) ** -0.5),
        "ssm_b_im": nrm(ks[12], (DEPTH, SSM_GROUPS, SSM_STATE, SSM_GROUP), (2.0 * SSM_GROUP) ** -0.5),
        "ssm_c_re": nrm(ks[13], (DEPTH, SSM_GROUPS, SSM_GROUP, SSM_STATE), (2.0 * SSM_STATE) ** -0.5),
        "ssm_c_im": nrm(ks[14], (DEPTH, SSM_GROUPS, SSM_GROUP, SSM_STATE), (2.0 * SSM_STATE) ** -0.5),
        "ssm_d": nrm(ks[15], (DEPTH, SSM_WIDTH), 1.0),
        "w_glu": nrm(ks[16], (DEPTH, SSM_WIDTH, SSM_WIDTH), SSM_WIDTH ** -0.5),
        "b_glu": nrm(ks[17], (DEPTH, SSM_WIDTH), 0.01),
        "w_out": nrm(ks[18], (DEPTH, MIX_WIDTH, D_MODEL), MIX_WIDTH ** -0.5),
        "final_norm_g": 1.0 + nrm(ks[19], (D_MODEL,), 0.01),
    }


def reference(x_prompt, x_sample, state_ret, state_ssm_re, state_ssm_im, norm_g, w_in, ret_norm_g,
              ssm_lambda_re, ssm_lambda_im, ssm_log_step, ssm_b_re, ssm_b_im, ssm_c_re, ssm_c_im,
              ssm_d, w_glu, b_glu, w_out, final_norm_g):
    f32 = jnp.float32
    bp, tp, _ = x_prompt.shape
    bs, ts, _ = x_sample.shape
    pos_p = jnp.arange(tp, dtype=f32)
    pos_s = PAST_LEN + jnp.arange(ts, dtype=f32)
    xp, xs = x_prompt, x_sample
    rp, hp_re, hp_im, rs, hs_re, hs_im = [], [], [], [], [], []
    for l in range(DEPTH):
        w = (norm_g[l], w_in[l], ret_norm_g[l], ssm_lambda_re[l], ssm_lambda_im[l], ssm_log_step[l],
             ssm_b_re[l], ssm_b_im[l], ssm_c_re[l], ssm_c_im[l], ssm_d[l], w_glu[l], b_glu[l], w_out[l])
        ret0_p = jnp.zeros((bp, RET_HEADS, RET_HEAD_DIM, RET_HEAD_DIM), f32)
        h0_p = jnp.zeros((bp, SSM_GROUPS, SSM_STATE), jnp.complex64)
        xp, r_p, h_p = _layer(xp, pos_p, ret0_p, h0_p, *w)
        h0_s = lax.complex(state_ssm_re[l].astype(f32), state_ssm_im[l].astype(f32))
        xs, r_s, h_s = _layer(xs, pos_s, state_ret[l], h0_s, *w)
        rp.append(r_p); hp_re.append(h_p.real); hp_im.append(h_p.imag)
        rs.append(r_s); hs_re.append(h_s.real); hs_im.append(h_s.imag)
    y_prompt = _rmsnorm(xp, final_norm_g).astype(x_prompt.dtype)
    y_sample = _rmsnorm(xs, final_norm_g).astype(x_sample.dtype)
    return (y_prompt, y_sample, jnp.stack(rp), jnp.stack(hp_re), jnp.stack(hp_im),
            jnp.stack(rs), jnp.stack(hs_re), jnp.stack(hs_im))
```

```python
import functools
import math

import jax
import jax.numpy as jnp
import numpy as np
from jax import lax
from jax.experimental import pallas as pl
from jax.experimental.pallas import tpu as pltpu

D_MODEL = 1024
RET_HEADS = 4
HEAD_DIM = 128
RET_WIDTH = RET_HEADS * HEAD_DIM
SSM_WIDTH = 512
SSM_GROUP = 16
SSM_GROUPS = 32
SSM_STATE = 64
ROPE_BASE = 10000.0
EPS = 1e-6
LAMBDA_RE_MAX = -1e-4
PAST_LEN = 2048
RET_CHUNK = 64

SSM_HALVES = 2
HALF_IN = SSM_WIDTH // SSM_HALVES
HALF_RI = (SSM_GROUPS // SSM_HALVES) * SSM_STATE
HALF_STATE = 2 * HALF_RI
N_STATE = SSM_HALVES * HALF_STATE
SCAN_LANES = 512

VMEM_LIMIT_BYTES = 56 * 1024 * 1024

F32 = jnp.float32
BF16 = jnp.bfloat16


def _rms(x, g):
    return x * lax.rsqrt(jnp.mean(x * x, axis=-1, keepdims=True) + EPS) * g


def _u_kernel(x_ref, g_ref, wu_ref, u_ref):
    nb, tb, d = x_ref.shape
    hn = _rms(x_ref[...].reshape(nb * tb, d), g_ref[...]).astype(BF16)
    u = jnp.dot(hn, wu_ref[...], preferred_element_type=F32)
    u_ref[...] = u.reshape(nb, tb, SSM_WIDTH)


def _u_call(x, norm_g, w_u, nb, tb):
    b, t, d = x.shape
    const = lambda i, j: (0, 0)
    return pl.pallas_call(
        _u_kernel,
        grid=(b // nb, t // tb),
        in_specs=[
            pl.BlockSpec((nb, tb, d), lambda i, j: (i, j, 0)),
            pl.BlockSpec((1, d), const),
            pl.BlockSpec((d, SSM_WIDTH), const),
        ],
        out_specs=pl.BlockSpec((nb, tb, SSM_WIDTH), lambda i, j: (i, j, 0)),
        out_shape=jax.ShapeDtypeStruct((b, t, SSM_WIDTH), F32),
        compiler_params=pltpu.CompilerParams(
            dimension_semantics=("arbitrary", "arbitrary"), vmem_limit_bytes=VMEM_LIMIT_BYTES),
        name="ssm_input_proj",
    )(x, norm_g, w_u)


def _ssm_kernel(u_ref, h0_ref, ar_ref, ai_ref, bd_ref, cd_ref, d_ref, y_ref, hfin_ref,
                utb_ref, bu_ref, hs_ref, ytb_ref):
    nb, tb, w = u_ref.shape
    rows = nb * tb

    @pl.when(pl.program_id(0) == 0)
    def _():
        hfin_ref[...] = h0_ref[...]

    for t in range(tb):
        utb_ref[t] = u_ref[:, t, :]
    u = utb_ref[...].reshape(rows, w)
    ub = u.astype(BF16)
    for h in range(SSM_HALVES):
        bu_ref[:, h * HALF_STATE:(h + 1) * HALF_STATE] = jnp.dot(
            ub[:, h * HALF_IN:(h + 1) * HALF_IN], bd_ref[h], preferred_element_type=F32)

    for h in range(SSM_HALVES):
        for q in range(HALF_RI // SCAN_LANES):
            re = h * HALF_STATE + q * SCAN_LANES
            im = re + HALF_RI
            ar = jnp.broadcast_to(ar_ref[h:h + 1, q * SCAN_LANES:(q + 1) * SCAN_LANES], (nb, SCAN_LANES))
            ai = jnp.broadcast_to(ai_ref[h:h + 1, q * SCAN_LANES:(q + 1) * SCAN_LANES], (nb, SCAN_LANES))

            def step(t, carry, re=re, im=im, ar=ar, ai=ai):
                hr, hi = carry
                r0 = pl.multiple_of(t * nb, nb)
                nr = ar * hr - ai * hi + bu_ref[pl.ds(r0, nb), re:re + SCAN_LANES]
                ni = ar * hi + ai * hr + bu_ref[pl.ds(r0, nb), im:im + SCAN_LANES]
                hs_ref[pl.ds(r0, nb), re:re + SCAN_LANES] = nr.astype(BF16)
                hs_ref[pl.ds(r0, nb), im:im + SCAN_LANES] = ni.astype(BF16)
                return nr, ni

            hr, hi = lax.fori_loop(
                0, tb, step,
                (hfin_ref[:, re:re + SCAN_LANES], hfin_ref[:, im:im + SCAN_LANES]), unroll=4)
            hfin_ref[:, re:re + SCAN_LANES] = hr
            hfin_ref[:, im:im + SCAN_LANES] = hi

    for h in range(SSM_HALVES):
        yh = jnp.dot(hs_ref[:, h * HALF_STATE:(h + 1) * HALF_STATE], cd_ref[h],
                     preferred_element_type=F32)
        cols = slice(h * HALF_IN, (h + 1) * HALF_IN)
        ytb_ref[:, :, cols] = (yh + d_ref[:, cols] * u[:, cols]).reshape(tb, nb, HALF_IN)
    for b in range(nb):
        y_ref[b] = ytb_ref[:, b, :]


def _ssm_call(u, h0, ar, ai, bd, cd, d, tb):
    nb, t, w = u.shape
    rows = nb * tb
    c2 = lambda i: (0, 0)
    c3 = lambda i: (0, 0, 0)
    return pl.pallas_call(
        _ssm_kernel,
        grid=(t // tb,),
        in_specs=[
            pl.BlockSpec((nb, tb, w), lambda i: (0, i, 0)),
            pl.BlockSpec((nb, N_STATE), c2),
            pl.BlockSpec((SSM_HALVES, HALF_RI), c2),
            pl.BlockSpec((SSM_HALVES, HALF_RI), c2),
            pl.BlockSpec((SSM_HALVES, HALF_IN, HALF_STATE), c3),
            pl.BlockSpec((SSM_HALVES, HALF_STATE, HALF_IN), c3),
            pl.BlockSpec((1, w), c2),
        ],
        out_specs=[
            pl.BlockSpec((nb, tb, w), lambda i: (0, i, 0)),
            pl.BlockSpec((nb, N_STATE), c2),
        ],
        out_shape=[
            jax.ShapeDtypeStruct((nb, t, w), F32),
            jax.ShapeDtypeStruct((nb, N_STATE), F32),
        ],
        scratch_shapes=[
            pltpu.VMEM((tb, nb, w), F32),
            pltpu.VMEM((rows, N_STATE), F32),
            pltpu.VMEM((rows, N_STATE), BF16),
            pltpu.VMEM((tb, nb, w), F32),
        ],
        compiler_params=pltpu.CompilerParams(
            dimension_semantics=("arbitrary",), vmem_limit_bytes=VMEM_LIMIT_BYTES),
        name="ssm_scan",
    )(u, h0, ar, ai, bd, cd, d)


def _layer_kernel(chunk, x_ref, y_ref, cos_ref, sin_ref, s0_ref, ng_ref, w1_ref, rg_ref,
                  dm_ref, qd_ref, kd_ref, sd_ref, wglu_ref, bglu_ref, wo_ref, fg_ref,
                  out_ref, sfin_ref, o_ref):
    nb, tb, d = x_ref.shape
    rows = nb * tb
    n_chunks = tb // chunk

    @pl.when(pl.program_id(1) == 0)
    def _():
        sfin_ref[...] = s0_ref[...]

    x = x_ref[...].reshape(rows, d)
    hn = _rms(x, ng_ref[...]).astype(BF16)
    proj = jnp.dot(hn, w1_ref[...], preferred_element_type=F32)

    cos = cos_ref[...]
    sin = sin_ref[...]

    def rotary(a):
        return a * cos + pltpu.roll(a, HEAD_DIM // 2, 1) * sin

    for h in range(RET_HEADS):
        lanes = slice(h * HEAD_DIM, (h + 1) * HEAD_DIM)
        q = rotary(proj[:, lanes]).astype(BF16)
        k = rotary(proj[:, RET_WIDTH + h * HEAD_DIM:RET_WIDTH + (h + 1) * HEAD_DIM]) * (HEAD_DIM ** -0.5)
        kb = k.astype(BF16)
        vb = proj[:, 2 * RET_WIDTH + h * HEAD_DIM:2 * RET_WIDTH + (h + 1) * HEAD_DIM].astype(BF16)
        dm = dm_ref[h]
        qd = qd_ref[h]
        kd = kd_ref[h]
        sd = sd_ref[h]
        for n in range(nb):
            s = sfin_ref[n, h]
            for c in range(n_chunks):
                r = slice(n * tb + c * chunk, n * tb + (c + 1) * chunk)
                qc, kc, vc = q[r], kb[r], vb[r]
                scores = lax.dot_general(qc, kc, (((1,), (1,)), ((), ())), preferred_element_type=F32)
                p = (scores * dm).astype(BF16)
                o = jnp.dot(p, vc, preferred_element_type=F32)
                o = o + jnp.dot(qc, s.astype(BF16), preferred_element_type=F32) * qd
                kdt = (k[r] * kd).astype(BF16).T
                s = sd * s + jnp.dot(kdt, vc, preferred_element_type=F32)
                o_ref[r, lanes] = o
            sfin_ref[n, h] = s

    g_ret = proj[:, 3 * RET_WIDTH:4 * RET_WIDTH]
    g_ssm = proj[:, 4 * RET_WIDTH:4 * RET_WIDTH + SSM_WIDTH]
    rg = rg_ref[...]
    o_parts = []
    for h in range(RET_HEADS):
        lanes = slice(h * HEAD_DIM, (h + 1) * HEAD_DIM)
        o_parts.append(_rms(o_ref[:, lanes], rg[:, lanes]))
    o = jnp.concatenate(o_parts, axis=-1) * jax.nn.silu(g_ret)

    z = jax.nn.gelu(y_ref[...].reshape(rows, SSM_WIDTH))
    gate = jnp.dot(z.astype(BF16), wglu_ref[...], preferred_element_type=F32) + bglu_ref[...]
    z = z * jax.nn.sigmoid(gate)
    z = z * jax.nn.silu(g_ssm)

    mix = jnp.concatenate([o, z], axis=-1).astype(BF16)
    res = x + jnp.dot(mix, wo_ref[...], preferred_element_type=F32)
    out_ref[...] = _rms(res, fg_ref[...]).reshape(nb, tb, d)


def _layer_call(x, y, cos, sin, s0, norm_g, w1, ret_norm_g, dm, qd, kd, sd, w_glu, b_glu,
                w_out, final_g, nb, tb, chunk):
    b, t, d = x.shape
    rows = nb * tb
    c2 = lambda i, j: (0, 0)
    c3 = lambda i, j: (0, 0, 0)
    state_spec = pl.BlockSpec((nb, RET_HEADS, HEAD_DIM, HEAD_DIM), lambda i, j: (i, 0, 0, 0))
    return pl.pallas_call(
        functools.partial(_layer_kernel, chunk),
        grid=(b // nb, t // tb),
        in_specs=[
            pl.BlockSpec((nb, tb, d), lambda i, j: (i, j, 0)),
            pl.BlockSpec((nb, tb, SSM_WIDTH), lambda i, j: (i, j, 0)),
            pl.BlockSpec((rows, HEAD_DIM), lambda i, j: (j, 0)),
            pl.BlockSpec((rows, HEAD_DIM), lambda i, j: (j, 0)),
            state_spec,
            pl.BlockSpec((1, d), c2),
            pl.BlockSpec(w1.shape, c2),
            pl.BlockSpec((1, RET_WIDTH), c2),
            pl.BlockSpec((RET_HEADS, chunk, chunk), c3),
            pl.BlockSpec((RET_HEADS, chunk, HEAD_DIM), c3),
            pl.BlockSpec((RET_HEADS, chunk, HEAD_DIM), c3),
            pl.BlockSpec((RET_HEADS, HEAD_DIM, HEAD_DIM), c3),
            pl.BlockSpec((SSM_WIDTH, SSM_WIDTH), c2),
            pl.BlockSpec((1, SSM_WIDTH), c2),
            pl.BlockSpec((D_MODEL, D_MODEL), c2),
            pl.BlockSpec((1, d), c2),
        ],
        out_specs=[
            pl.BlockSpec((nb, tb, d), lambda i, j: (i, j, 0)),
            state_spec,
        ],
        out_shape=[
            jax.ShapeDtypeStruct((b, t, d), F32),
            jax.ShapeDtypeStruct((b, RET_HEADS, HEAD_DIM, HEAD_DIM), F32),
        ],
        scratch_shapes=[pltpu.VMEM((rows, RET_WIDTH), F32)],
        compiler_params=pltpu.CompilerParams(
            dimension_semantics=("arbitrary", "arbitrary"), vmem_limit_bytes=VMEM_LIMIT_BYTES),
        name="retention_glu_layer",
    )(x, y, cos, sin, s0, norm_g, w1, ret_norm_g, dm, qd, kd, sd, w_glu, b_glu, w_out, final_g)


def _rotary_tables(pos, reps):
    half = HEAD_DIM // 2
    inv = ROPE_BASE ** (-jnp.arange(half, dtype=F32) / half)
    ang = pos.astype(F32)[:, None] * inv[None, :]
    cos, sin = jnp.cos(ang), jnp.sin(ang)
    cos2 = jnp.concatenate([cos, cos], axis=-1)
    sin2 = jnp.concatenate([-sin, sin], axis=-1)
    return jnp.tile(cos2, (reps, 1)), jnp.tile(sin2, (reps, 1))


def _retention_tables(chunk):
    log_g = jnp.log1p(-jnp.power(2.0, -5.0 - jnp.arange(RET_HEADS, dtype=F32)))
    idx = jnp.arange(chunk, dtype=F32)
    diff = idx[:, None] - idx[None, :]
    causal = diff >= 0
    dm = jnp.where(causal[None], jnp.exp(jnp.where(causal, diff, 0.0)[None] * log_g[:, None, None]), 0.0)
    q_decay = jnp.exp((idx + 1.0)[None, :] * log_g[:, None])
    k_decay = jnp.exp((chunk - 1.0 - idx)[None, :] * log_g[:, None])
    s_decay = jnp.exp(chunk * log_g)
    qd = jnp.broadcast_to(q_decay[:, :, None], (RET_HEADS, chunk, HEAD_DIM))
    kd = jnp.broadcast_to(k_decay[:, :, None], (RET_HEADS, chunk, HEAD_DIM))
    sd = jnp.broadcast_to(s_decay[:, None, None], (RET_HEADS, HEAD_DIM, HEAD_DIM))
    return dm, qd, kd, sd


def _ssm_tables(lambda_re, lambda_im, log_step, b_re, b_im, c_re, c_im):
    lam = lax.complex(jnp.minimum(lambda_re.astype(F32), LAMBDA_RE_MAX), lambda_im.astype(F32))
    dt = jnp.exp(log_step.astype(F32))[:, None]
    lam_bar = jnp.exp(lam * dt)
    b_bar = ((lam_bar - 1.0) / lam)[..., None] * lax.complex(b_re.astype(F32), b_im.astype(F32))
    gl = SSM_GROUPS // SSM_HALVES
    eye = jnp.eye(gl, dtype=F32)
    ar = jnp.real(lam_bar).reshape(SSM_HALVES, HALF_RI)
    ai = jnp.imag(lam_bar).reshape(SSM_HALVES, HALF_RI)

    def place_b(m):
        m = m.reshape(SSM_HALVES, gl, SSM_STATE, SSM_GROUP)
        return jnp.einsum("hgpc,gk->hgckp", m, eye).reshape(SSM_HALVES, HALF_IN, HALF_RI)

    def place_c(m):
        m = m.reshape(SSM_HALVES, gl, SSM_GROUP, SSM_STATE)
        return jnp.einsum("hgcp,gk->hkpgc", m, eye).reshape(SSM_HALVES, HALF_RI, HALF_IN)

    bd = jnp.concatenate([place_b(jnp.real(b_bar)), place_b(jnp.imag(b_bar))], axis=-1).astype(BF16)
    cd = jnp.concatenate([place_c(c_re.astype(F32)), place_c(-c_im.astype(F32))], axis=1).astype(BF16)
    return ar, ai, bd, cd


def _state_to_lanes(re, im):
    b = re.shape[0]
    re = re.astype(F32).reshape(b, SSM_HALVES, HALF_RI)
    im = im.astype(F32).reshape(b, SSM_HALVES, HALF_RI)
    return jnp.concatenate([re, im], axis=-1).reshape(b, N_STATE)


def _lanes_to_state(h):
    b = h.shape[0]
    h = h.reshape(b, SSM_HALVES, 2, HALF_RI)
    return (h[:, :, 0].reshape(b, SSM_GROUPS, SSM_STATE), h[:, :, 1].reshape(b, SSM_GROUPS, SSM_STATE))


def _stream_step(x, pos, ret_s0, h0, weights, nb, tb, chunk, scan_tb):
    (norm_g, w1, w_u, ret_norm_g, ssm_tabs, ssm_d, w_glu, b_glu, w_out, final_g) = weights
    b, t, _ = x.shape
    ar, ai, bd, cd = ssm_tabs
    u = _u_call(x, norm_g, w_u, nb, tb)
    y, h_fin = _ssm_call(u, h0, ar, ai, bd, cd, ssm_d, scan_tb)
    cos, sin = _rotary_tables(pos, nb)
    dm, qd, kd, sd = _retention_tables(chunk)
    out, s_fin = _layer_call(x, y, cos, sin, ret_s0, norm_g, w1, ret_norm_g, dm, qd, kd, sd,
                             w_glu, b_glu, w_out, final_g, nb, tb, chunk)
    h_re, h_im = _lanes_to_state(h_fin)
    return out, s_fin, h_re, h_im


def kernel(x_prompt, x_sample, state_ret, state_ssm_re, state_ssm_im, norm_g, w_in, ret_norm_g,
           ssm_lambda_re, ssm_lambda_im, ssm_log_step, ssm_b_re, ssm_b_im, ssm_c_re, ssm_c_im,
           ssm_d, w_glu, b_glu, w_out, final_norm_g):
    depth = w_in.shape[0]
    assert depth == 1, "single-layer trunk"
    bp, tp, _ = x_prompt.shape
    bs, ts, _ = x_sample.shape
    l = 0
    w = w_in[l]
    u_lo = 4 * RET_WIDTH
    w1 = jnp.concatenate([w[:, :u_lo], w[:, u_lo + SSM_WIDTH:]], axis=1).astype(BF16)
    w_u = w[:, u_lo:u_lo + SSM_WIDTH].astype(BF16)
    weights = (
        norm_g[l][None, :].astype(F32), w1, w_u, ret_norm_g[l].reshape(1, RET_WIDTH).astype(F32),
        _ssm_tables(ssm_lambda_re[l], ssm_lambda_im[l], ssm_log_step[l], ssm_b_re[l], ssm_b_im[l],
                    ssm_c_re[l], ssm_c_im[l]),
        ssm_d[l][None, :].astype(F32), w_glu[l].astype(BF16), b_glu[l][None, :].astype(F32),
        w_out[l].astype(BF16), final_norm_g[None, :].astype(F32),
    )

    pos_p = jnp.arange(tp, dtype=F32)
    pos_s = PAST_LEN + jnp.arange(ts, dtype=F32)
    ret0_p = jnp.zeros((bp, RET_HEADS, HEAD_DIM, HEAD_DIM), F32)
    h0_p = jnp.zeros((bp, N_STATE), F32)
    y_p, r_p, hp_re, hp_im = _stream_step(
        x_prompt, pos_p, ret0_p, h0_p, weights, nb=1, tb=512, chunk=min(tp, RET_CHUNK), scan_tb=32)
    h0_s = _state_to_lanes(state_ssm_re[l], state_ssm_im[l])
    y_s, r_s, hs_re, hs_im = _stream_step(
        x_sample, pos_s, state_ret[l].astype(F32), h0_s, weights, nb=bs, tb=ts, chunk=min(ts, RET_CHUNK),
        scan_tb=ts)
    return (y_p, y_s, r_p[None], hp_re[None], hp_im[None], r_s[None], hs_re[None], hs_im[None])
```

```python
import functools
from typing import NamedTuple

import jax
import jax.numpy as jnp
from jax import lax
from jax.experimental import pallas as pl
from jax.experimental.pallas import tpu as pltpu

D_MODEL = 1024
RET_HEADS = 4
HEAD_DIM = 128
RET_WIDTH = RET_HEADS * HEAD_DIM
SSM_WIDTH = 512
SSM_GROUP = 16
SSM_GROUPS = 32
SSM_STATE = 64
ROPE_BASE = 10000.0
EPS = 1e-6
LAMBDA_RE_MAX = -1e-4
PAST_LEN = 2048

LAYER_ROWS = 512
LAYER_CHUNK = 256
SCAN_STEPS = 32

SSM_HALVES = 2
HALF_IN = SSM_WIDTH // SSM_HALVES
HALF_RI = (SSM_GROUPS // SSM_HALVES) * SSM_STATE
HALF_STATE = 2 * HALF_RI
N_STATE = SSM_HALVES * HALF_STATE
SCAN_LANES = 512

VMEM_LIMIT_BYTES = 56 * 1024 * 1024

F32 = jnp.float32
BF16 = jnp.bfloat16


def _rms(x, g):
    return x * lax.rsqrt(jnp.mean(x * x, axis=-1, keepdims=True) + EPS) * g


def _ssm_kernel(x_ref, g_ref, wu_ref, h0_ref, ar_ref, ai_ref, bd_ref, cd_ref, d_ref, y_ref, hfin_ref,
                ubt_ref, utb_ref, bu_ref, hs_ref, ytb_ref):
    nb, tb, d_model = x_ref.shape
    w = SSM_WIDTH
    rows = nb * tb

    step = pl.program_id(0)

    @pl.when(step == 0)
    def _():
        ubt_ref[...] = jnp.zeros_like(ubt_ref)

    @pl.when(step <= 1)
    def _():
        hfin_ref[...] = h0_ref[...]

    for t in range(tb):
        utb_ref[t] = ubt_ref[:, t, :]
    u = utb_ref[...].reshape(rows, w)
    ub = u.astype(BF16)
    hn = _rms(x_ref[...].reshape(rows, d_model), g_ref[...]).astype(BF16)
    ubt_ref[...] = jnp.dot(hn, wu_ref[...], preferred_element_type=F32).reshape(nb, tb, w)

    for h in range(SSM_HALVES):
        cols = slice(h * HALF_IN, (h + 1) * HALF_IN)
        yh = d_ref[:, cols] * u[:, cols]
        for q in range(HALF_RI // SCAN_LANES):
            re_h = q * SCAN_LANES
            im_h = HALF_RI + re_h
            re = h * HALF_STATE + re_h
            im = h * HALF_STATE + im_h
            bu_ref[:, re:re + SCAN_LANES] = jnp.dot(
                ub[:, cols], bd_ref[h, :, re_h:re_h + SCAN_LANES], preferred_element_type=F32)
            bu_ref[:, im:im + SCAN_LANES] = jnp.dot(
                ub[:, cols], bd_ref[h, :, im_h:im_h + SCAN_LANES], preferred_element_type=F32)
            ar = jnp.broadcast_to(ar_ref[h:h + 1, re_h:re_h + SCAN_LANES], (nb, SCAN_LANES))
            ai = jnp.broadcast_to(ai_ref[h:h + 1, re_h:re_h + SCAN_LANES], (nb, SCAN_LANES))
            hr = hfin_ref[:, re:re + SCAN_LANES]
            hi = hfin_ref[:, im:im + SCAN_LANES]
            for t in range(tb):
                r = slice(t * nb, (t + 1) * nb)
                hr, hi = (ar * hr - ai * hi + bu_ref[r, re:re + SCAN_LANES],
                          ar * hi + ai * hr + bu_ref[r, im:im + SCAN_LANES])
                hs_ref[r, re:re + SCAN_LANES] = hr.astype(BF16)
                hs_ref[r, im:im + SCAN_LANES] = hi.astype(BF16)
            hfin_ref[:, re:re + SCAN_LANES] = hr
            hfin_ref[:, im:im + SCAN_LANES] = hi
            yh = yh + jnp.dot(hs_ref[:, re:re + SCAN_LANES], cd_ref[h, re_h:re_h + SCAN_LANES, :],
                              preferred_element_type=F32)
            yh = yh + jnp.dot(hs_ref[:, im:im + SCAN_LANES], cd_ref[h, im_h:im_h + SCAN_LANES, :],
                              preferred_element_type=F32)
        ytb_ref[:, :, cols] = yh.reshape(tb, nb, HALF_IN)
    for b in range(nb):
        y_ref[b] = ytb_ref[:, b, :]


def _ssm_call(x, norm_g, w_u, h0, ar, ai, bd, cd, d, tb):
    nb, t, d_model = x.shape
    w = SSM_WIDTH
    rows = nb * tb
    n_blocks = t // tb
    c2 = lambda i: (0, 0)
    c3 = lambda i: (0, 0, 0)
    return pl.pallas_call(
        _ssm_kernel,
        grid=(n_blocks + 1,),
        in_specs=[
            pl.BlockSpec((nb, tb, d_model), lambda i: (0, jnp.minimum(i, n_blocks - 1), 0)),
            pl.BlockSpec((1, d_model), c2),
            pl.BlockSpec((d_model, w), c2),
            pl.BlockSpec((nb, N_STATE), c2),
            pl.BlockSpec((SSM_HALVES, HALF_RI), c2),
            pl.BlockSpec((SSM_HALVES, HALF_RI), c2),
            pl.BlockSpec((SSM_HALVES, HALF_IN, HALF_STATE), c3),
            pl.BlockSpec((SSM_HALVES, HALF_STATE, HALF_IN), c3),
            pl.BlockSpec((1, w), c2),
        ],
        out_specs=[
            pl.BlockSpec((nb, tb, w), lambda i: (0, jnp.maximum(i - 1, 0), 0)),
            pl.BlockSpec((nb, N_STATE), c2),
        ],
        out_shape=[
            jax.ShapeDtypeStruct((nb, t, w), F32),
            jax.ShapeDtypeStruct((nb, N_STATE), F32),
        ],
        scratch_shapes=[
            pltpu.VMEM((nb, tb, w), F32),
            pltpu.VMEM((tb, nb, w), F32),
            pltpu.VMEM((rows, N_STATE), F32),
            pltpu.VMEM((rows, N_STATE), BF16),
            pltpu.VMEM((tb, nb, w), F32),
        ],
        compiler_params=pltpu.CompilerParams(
            dimension_semantics=("arbitrary",), vmem_limit_bytes=VMEM_LIMIT_BYTES),
        name="ssm_scan",
    )(x, norm_g, w_u, h0, ar, ai, bd, cd, d)


def _layer_kernel(chunk, sub, x_ref, y_ref, cos_ref, sin_ref, s0_ref, ng_ref, w1_ref, rg_ref,
                  dm_ref, qd_ref, kd_ref, sd_ref, wglu_ref, bglu_ref, wo_ref, fg_ref,
                  out_ref, sfin_ref, o_ref):
    nb, tb, d = x_ref.shape
    rows = nb * tb

    @pl.when(pl.program_id(1) == 0)
    def _():
        sfin_ref[...] = s0_ref[...]

    def load_rows(ref, r0):
        if nb == 1:
            return ref[0, r0:r0 + sub, :]
        return ref[r0 // tb:(r0 + sub) // tb].reshape(sub, ref.shape[-1])

    def store_rows(ref, r0, val):
        if nb == 1:
            ref[0, r0:r0 + sub, :] = val
        else:
            ref[r0 // tb:(r0 + sub) // tb] = val.reshape(sub // tb, tb, ref.shape[-1])

    rg = rg_ref[...]
    for r0 in range(0, rows, sub):
        x = load_rows(x_ref, r0)
        hn = _rms(x, ng_ref[...]).astype(BF16)
        proj = jnp.dot(hn, w1_ref[...], preferred_element_type=F32)
        cos = cos_ref[r0:r0 + sub, :]
        sin = sin_ref[r0:r0 + sub, :]

        def rotary(a):
            return a * cos + pltpu.roll(a, HEAD_DIM // 2, 1) * sin

        for h in range(RET_HEADS):
            lanes = slice(h * HEAD_DIM, (h + 1) * HEAD_DIM)
            q = rotary(proj[:, lanes]).astype(BF16)
            k = rotary(proj[:, RET_WIDTH + h * HEAD_DIM:RET_WIDTH + (h + 1) * HEAD_DIM]) * (HEAD_DIM ** -0.5)
            kb = k.astype(BF16)
            vb = proj[:, 2 * RET_WIDTH + h * HEAD_DIM:2 * RET_WIDTH + (h + 1) * HEAD_DIM].astype(BF16)
            dm = dm_ref[h]
            qd = qd_ref[h]
            kd = kd_ref[h]
            sd = sd_ref[h]
            for c0 in range(0, sub, chunk):
                n = (r0 + c0) // tb
                r = slice(c0, c0 + chunk)
                qc, kc, vc = q[r], kb[r], vb[r]
                s = sfin_ref[n, h]
                scores = lax.dot_general(qc, kc, (((1,), (1,)), ((), ())), preferred_element_type=F32)
                p = (scores * dm).astype(BF16)
                o = jnp.dot(p, vc, preferred_element_type=F32)
                o = o + jnp.dot(qc, s.astype(BF16), preferred_element_type=F32) * qd
                kdt = (k[r] * kd).astype(BF16).T
                sfin_ref[n, h] = sd * s + jnp.dot(kdt, vc, preferred_element_type=F32)
                o_ref[r0 + c0:r0 + c0 + chunk, lanes] = o

        g_ret = proj[:, 3 * RET_WIDTH:4 * RET_WIDTH]
        g_ssm = proj[:, 4 * RET_WIDTH:4 * RET_WIDTH + SSM_WIDTH]
        o_parts = []
        for h in range(RET_HEADS):
            lanes = slice(h * HEAD_DIM, (h + 1) * HEAD_DIM)
            o_parts.append(_rms(o_ref[r0:r0 + sub, lanes], rg[:, lanes]))
        o = jnp.concatenate(o_parts, axis=-1) * jax.nn.silu(g_ret)

        z = jax.nn.gelu(load_rows(y_ref, r0))
        gate = jnp.dot(z.astype(BF16), wglu_ref[...], preferred_element_type=F32) + bglu_ref[...]
        z = z * jax.nn.sigmoid(gate)
        z = z * jax.nn.silu(g_ssm)

        mix = jnp.concatenate([o, z], axis=-1).astype(BF16)
        res = x + jnp.dot(mix, wo_ref[...], preferred_element_type=F32)
        store_rows(out_ref, r0, _rms(res, fg_ref[...]))


def _layer_call(x, y, cos, sin, s0, norm_g, w1, ret_norm_g, dm, qd, kd, sd, w_glu, b_glu,
                w_out, final_g, nb, tb, chunk, sub):
    b, t, d = x.shape
    rows = nb * tb
    c2 = lambda i, j: (0, 0)
    c3 = lambda i, j: (0, 0, 0)
    state_spec = pl.BlockSpec((nb, RET_HEADS, HEAD_DIM, HEAD_DIM), lambda i, j: (i, 0, 0, 0))
    return pl.pallas_call(
        functools.partial(_layer_kernel, chunk, sub),
        grid=(b // nb, t // tb),
        in_specs=[
            pl.BlockSpec((nb, tb, d), lambda i, j: (i, j, 0)),
            pl.BlockSpec((nb, tb, SSM_WIDTH), lambda i, j: (i, j, 0)),
            pl.BlockSpec((rows, HEAD_DIM), lambda i, j: (j, 0)),
            pl.BlockSpec((rows, HEAD_DIM), lambda i, j: (j, 0)),
            state_spec,
            pl.BlockSpec((1, d), c2),
            pl.BlockSpec(w1.shape, c2),
            pl.BlockSpec((1, RET_WIDTH), c2),
            pl.BlockSpec((RET_HEADS, chunk, chunk), c3),
            pl.BlockSpec((RET_HEADS, chunk, HEAD_DIM), c3),
            pl.BlockSpec((RET_HEADS, chunk, HEAD_DIM), c3),
            pl.BlockSpec((RET_HEADS, HEAD_DIM, HEAD_DIM), c3),
            pl.BlockSpec((SSM_WIDTH, SSM_WIDTH), c2),
            pl.BlockSpec((1, SSM_WIDTH), c2),
            pl.BlockSpec((D_MODEL, D_MODEL), c2),
            pl.BlockSpec((1, d), c2),
        ],
        out_specs=[
            pl.BlockSpec((nb, tb, d), lambda i, j: (i, j, 0)),
            state_spec,
        ],
        out_shape=[
            jax.ShapeDtypeStruct((b, t, d), F32),
            jax.ShapeDtypeStruct((b, RET_HEADS, HEAD_DIM, HEAD_DIM), F32),
        ],
        scratch_shapes=[pltpu.VMEM((rows, RET_WIDTH), F32)],
        compiler_params=pltpu.CompilerParams(
            dimension_semantics=("arbitrary", "arbitrary"), vmem_limit_bytes=VMEM_LIMIT_BYTES),
        name="retention_glu_layer",
    )(x, y, cos, sin, s0, norm_g, w1, ret_norm_g, dm, qd, kd, sd, w_glu, b_glu, w_out, final_g)


def _rotary_tables(pos, reps):
    half = HEAD_DIM // 2
    inv = ROPE_BASE ** (-jnp.arange(half, dtype=F32) / half)
    ang = pos.astype(F32)[:, None] * inv[None, :]
    cos, sin = jnp.cos(ang), jnp.sin(ang)
    cos2 = jnp.concatenate([cos, cos], axis=-1)
    sin2 = jnp.concatenate([-sin, sin], axis=-1)
    return jnp.tile(cos2, (reps, 1)), jnp.tile(sin2, (reps, 1))


def _retention_tables(chunk):
    log_g = jnp.log1p(-jnp.power(2.0, -5.0 - jnp.arange(RET_HEADS, dtype=F32)))
    idx = jnp.arange(chunk, dtype=F32)
    diff = idx[:, None] - idx[None, :]
    causal = diff >= 0
    dm = jnp.where(causal[None], jnp.exp(jnp.where(causal, diff, 0.0)[None] * log_g[:, None, None]), 0.0)
    q_decay = jnp.exp((idx + 1.0)[None, :] * log_g[:, None])
    k_decay = jnp.exp((chunk - 1.0 - idx)[None, :] * log_g[:, None])
    s_decay = jnp.exp(chunk * log_g)
    qd = jnp.broadcast_to(q_decay[:, :, None], (RET_HEADS, chunk, HEAD_DIM))
    kd = jnp.broadcast_to(k_decay[:, :, None], (RET_HEADS, chunk, HEAD_DIM))
    sd = jnp.broadcast_to(s_decay[:, None, None], (RET_HEADS, HEAD_DIM, HEAD_DIM))
    return dm, qd, kd, sd


def _ssm_tables(lambda_re, lambda_im, log_step, b_re, b_im, c_re, c_im):
    lam = lax.complex(jnp.minimum(lambda_re.astype(F32), LAMBDA_RE_MAX), lambda_im.astype(F32))
    dt = jnp.exp(log_step.astype(F32))[:, None]
    lam_bar = jnp.exp(lam * dt)
    b_bar = ((lam_bar - 1.0) / lam)[..., None] * lax.complex(b_re.astype(F32), b_im.astype(F32))
    gl = SSM_GROUPS // SSM_HALVES
    eye = jnp.eye(gl, dtype=F32)
    ar = jnp.real(lam_bar).reshape(SSM_HALVES, HALF_RI)
    ai = jnp.imag(lam_bar).reshape(SSM_HALVES, HALF_RI)

    def place_b(m):
        m = m.reshape(SSM_HALVES, gl, SSM_STATE, SSM_GROUP)
        return jnp.einsum("hgpc,gk->hgckp", m, eye).reshape(SSM_HALVES, HALF_IN, HALF_RI)

    def place_c(m):
        m = m.reshape(SSM_HALVES, gl, SSM_GROUP, SSM_STATE)
        return jnp.einsum("hgcp,gk->hkpgc", m, eye).reshape(SSM_HALVES, HALF_RI, HALF_IN)

    bd = jnp.concatenate([place_b(jnp.real(b_bar)), place_b(jnp.imag(b_bar))], axis=-1).astype(BF16)
    cd = jnp.concatenate([place_c(c_re.astype(F32)), place_c(-c_im.astype(F32))], axis=1).astype(BF16)
    return ar, ai, bd, cd


def _state_to_lanes(re, im):
    b = re.shape[0]
    re = re.astype(F32).reshape(b, SSM_HALVES, HALF_RI)
    im = im.astype(F32).reshape(b, SSM_HALVES, HALF_RI)
    return jnp.concatenate([re, im], axis=-1).reshape(b, N_STATE)


def _lanes_to_state(h):
    b = h.shape[0]
    h = h.reshape(b, SSM_HALVES, 2, HALF_RI)
    return (h[:, :, 0].reshape(b, SSM_GROUPS, SSM_STATE), h[:, :, 1].reshape(b, SSM_GROUPS, SSM_STATE))


class _Plan(NamedTuple):
    nb: int
    tb: int
    chunk: int
    sub: int
    scan_tb: int


def _plan(b, t):
    if t >= LAYER_ROWS:
        return _Plan(nb=1, tb=LAYER_ROWS, chunk=LAYER_CHUNK, sub=LAYER_CHUNK, scan_tb=SCAN_STEPS)
    nb = min(b, LAYER_ROWS // t)
    return _Plan(nb=nb, tb=t, chunk=t, sub=nb * t, scan_tb=t)


def _stream_step(x, pos, ret_s0, h0, weights):
    (norm_g, w1, w_u, ret_norm_g, ssm_tabs, ssm_d, w_glu, b_glu, w_out, final_g) = weights
    b, t, _ = x.shape
    nb, tb, chunk, sub, scan_tb = _plan(b, t)
    ar, ai, bd, cd = ssm_tabs
    y, h_fin = _ssm_call(x, norm_g, w_u, h0, ar, ai, bd, cd, ssm_d, scan_tb)
    cos, sin = _rotary_tables(pos, nb)
    dm, qd, kd, sd = _retention_tables(chunk)
    out, s_fin = _layer_call(x, y, cos, sin, ret_s0, norm_g, w1, ret_norm_g, dm, qd, kd, sd,
                             w_glu, b_glu, w_out, final_g, nb, tb, chunk, sub)
    h_re, h_im = _lanes_to_state(h_fin)
    return out, s_fin, h_re, h_im


def kernel(x_prompt, x_sample, state_ret, state_ssm_re, state_ssm_im, norm_g, w_in, ret_norm_g,
           ssm_lambda_re, ssm_lambda_im, ssm_log_step, ssm_b_re, ssm_b_im, ssm_c_re, ssm_c_im,
           ssm_d, w_glu, b_glu, w_out, final_norm_g):
    depth = w_in.shape[0]
    assert depth == 1, "single-layer trunk"
    bp, tp, _ = x_prompt.shape
    bs, ts, _ = x_sample.shape
    l = 0
    w = w_in[l]
    u_lo = 4 * RET_WIDTH
    w1 = jnp.concatenate([w[:, :u_lo], w[:, u_lo + SSM_WIDTH:]], axis=1).astype(BF16)
    w_u = w[:, u_lo:u_lo + SSM_WIDTH].astype(BF16)
    weights = (
        norm_g[l][None, :].astype(F32), w1, w_u, ret_norm_g[l].reshape(1, RET_WIDTH).astype(F32),
        _ssm_tables(ssm_lambda_re[l], ssm_lambda_im[l], ssm_log_step[l], ssm_b_re[l], ssm_b_im[l],
                    ssm_c_re[l], ssm_c_im[l]),
        ssm_d[l][None, :].astype(F32), w_glu[l].astype(BF16), b_glu[l][None, :].astype(F32),
        w_out[l].astype(BF16), final_norm_g[None, :].astype(F32),
    )

    pos_p = jnp.arange(tp, dtype=F32)
    pos_s = PAST_LEN + jnp.arange(ts, dtype=F32)
    ret0_p = jnp.zeros((bp, RET_HEADS, HEAD_DIM, HEAD_DIM), F32)
    h0_p = jnp.zeros((bp, N_STATE), F32)
    y_p, r_p, hp_re, hp_im = _stream_step(x_prompt, pos_p, ret0_p, h0_p, weights)
    h0_s = _state_to_lanes(state_ssm_re[l], state_ssm_im[l])
    y_s, r_s, hs_re, hs_im = _stream_step(x_sample, pos_s, state_ret[l].astype(F32), h0_s, weights)
    return (y_p, y_s, r_p[None], hp_re[None], hp_im[None], r_s[None], hs_re[None], hs_im[None])
```

```python
import functools
from typing import NamedTuple

import jax
import jax.numpy as jnp
from jax import lax
from jax.experimental import pallas as pl
from jax.experimental.pallas import tpu as pltpu

D_MODEL = 1024
RET_HEADS = 4
HEAD_DIM = 128
RET_WIDTH = RET_HEADS * HEAD_DIM
SSM_WIDTH = 512
SSM_GROUP = 16
SSM_GROUPS = 32
SSM_STATE = 64
ROPE_BASE = 10000.0
EPS = 1e-6
LAMBDA_RE_MAX = -1e-4
PAST_LEN = 2048

LAYER_ROWS = 512
LAYER_CHUNK = 256
TOEP_CHUNK = 128
TOEP_LANES = SSM_GROUP * TOEP_CHUNK
TOEP_PROJ_STEPS = 1024

SSM_HALVES = 2
HALF_IN = SSM_WIDTH // SSM_HALVES
HALF_RI = (SSM_GROUPS // SSM_HALVES) * SSM_STATE
HALF_STATE = 2 * HALF_RI
N_STATE = SSM_HALVES * HALF_STATE
SCAN_LANES = 512

VMEM_LIMIT_BYTES = 56 * 1024 * 1024

F32 = jnp.float32
BF16 = jnp.bfloat16


def _rms(x, g):
    return x * lax.rsqrt(jnp.mean(x * x, axis=-1, keepdims=True) + EPS) * g


def _ssm_kernel(x_ref, g_ref, wu_ref, h0_ref, ar_ref, ai_ref, bd_ref, cd_ref, d_ref, y_ref, hfin_ref,
                ubt_ref, utb_ref, bu_ref, hs_ref, ytb_ref):
    nb, tb, d_model = x_ref.shape
    w = SSM_WIDTH
    rows = nb * tb

    step = pl.program_id(0)

    @pl.when(step == 0)
    def _():
        ubt_ref[...] = jnp.zeros_like(ubt_ref)

    @pl.when(step <= 1)
    def _():
        hfin_ref[...] = h0_ref[...]

    for t in range(tb):
        utb_ref[t] = ubt_ref[:, t, :]
    u = utb_ref[...].reshape(rows, w)
    ub = u.astype(BF16)
    hn = _rms(x_ref[...].reshape(rows, d_model), g_ref[...]).astype(BF16)
    ubt_ref[...] = jnp.dot(hn, wu_ref[...], preferred_element_type=F32).reshape(nb, tb, w)

    for h in range(SSM_HALVES):
        cols = slice(h * HALF_IN, (h + 1) * HALF_IN)
        yh = d_ref[:, cols] * u[:, cols]
        for q in range(HALF_RI // SCAN_LANES):
            re_h = q * SCAN_LANES
            im_h = HALF_RI + re_h
            re = h * HALF_STATE + re_h
            im = h * HALF_STATE + im_h
            bu_ref[:, re:re + SCAN_LANES] = jnp.dot(
                ub[:, cols], bd_ref[h, :, re_h:re_h + SCAN_LANES], preferred_element_type=F32)
            bu_ref[:, im:im + SCAN_LANES] = jnp.dot(
                ub[:, cols], bd_ref[h, :, im_h:im_h + SCAN_LANES], preferred_element_type=F32)
            ar = jnp.broadcast_to(ar_ref[h:h + 1, re_h:re_h + SCAN_LANES], (nb, SCAN_LANES))
            ai = jnp.broadcast_to(ai_ref[h:h + 1, re_h:re_h + SCAN_LANES], (nb, SCAN_LANES))
            hr = hfin_ref[:, re:re + SCAN_LANES]
            hi = hfin_ref[:, im:im + SCAN_LANES]
            for t in range(tb):
                r = slice(t * nb, (t + 1) * nb)
                hr, hi = (ar * hr - ai * hi + bu_ref[r, re:re + SCAN_LANES],
                          ar * hi + ai * hr + bu_ref[r, im:im + SCAN_LANES])
                hs_ref[r, re:re + SCAN_LANES] = hr.astype(BF16)
                hs_ref[r, im:im + SCAN_LANES] = hi.astype(BF16)
            hfin_ref[:, re:re + SCAN_LANES] = hr
            hfin_ref[:, im:im + SCAN_LANES] = hi
            yh = yh + jnp.dot(hs_ref[:, re:re + SCAN_LANES], cd_ref[h, re_h:re_h + SCAN_LANES, :],
                              preferred_element_type=F32)
            yh = yh + jnp.dot(hs_ref[:, im:im + SCAN_LANES], cd_ref[h, im_h:im_h + SCAN_LANES, :],
                              preferred_element_type=F32)
        ytb_ref[:, :, cols] = yh.reshape(tb, nb, HALF_IN)
    for b in range(nb):
        y_ref[b] = ytb_ref[:, b, :]


def _ssm_call(x, norm_g, w_u, h0, ar, ai, bd, cd, d, tb):
    nb, t, d_model = x.shape
    w = SSM_WIDTH
    rows = nb * tb
    n_blocks = t // tb
    c2 = lambda i: (0, 0)
    c3 = lambda i: (0, 0, 0)
    return pl.pallas_call(
        _ssm_kernel,
        grid=(n_blocks + 1,),
        in_specs=[
            pl.BlockSpec((nb, tb, d_model), lambda i: (0, jnp.minimum(i, n_blocks - 1), 0)),
            pl.BlockSpec((1, d_model), c2),
            pl.BlockSpec((d_model, w), c2),
            pl.BlockSpec((nb, N_STATE), c2),
            pl.BlockSpec((SSM_HALVES, HALF_RI), c2),
            pl.BlockSpec((SSM_HALVES, HALF_RI), c2),
            pl.BlockSpec((SSM_HALVES, HALF_IN, HALF_STATE), c3),
            pl.BlockSpec((SSM_HALVES, HALF_STATE, HALF_IN), c3),
            pl.BlockSpec((1, w), c2),
        ],
        out_specs=[
            pl.BlockSpec((nb, tb, w), lambda i: (0, jnp.maximum(i - 1, 0), 0)),
            pl.BlockSpec((nb, N_STATE), c2),
        ],
        out_shape=[
            jax.ShapeDtypeStruct((nb, t, w), F32),
            jax.ShapeDtypeStruct((nb, N_STATE), F32),
        ],
        scratch_shapes=[
            pltpu.VMEM((nb, tb, w), F32),
            pltpu.VMEM((tb, nb, w), F32),
            pltpu.VMEM((rows, N_STATE), F32),
            pltpu.VMEM((rows, N_STATE), BF16),
            pltpu.VMEM((tb, nb, w), F32),
        ],
        compiler_params=pltpu.CompilerParams(
            dimension_semantics=("arbitrary",), vmem_limit_bytes=VMEM_LIMIT_BYTES),
        name="ssm_scan",
    )(x, norm_g, w_u, h0, ar, ai, bd, cd, d)


def _ut_kernel(x_ref, g_ref, wut_ref, o_ref):
    tb = x_ref.shape[1]
    nk = tb // TOEP_CHUNK
    hn = _rms(x_ref[0], g_ref[...]).astype(BF16)
    ut = lax.dot_general(wut_ref[...], hn, (((1,), (1,)), ((), ())), preferred_element_type=F32)
    for g in range(SSM_GROUPS):
        rows = slice(g * SSM_GROUP, (g + 1) * SSM_GROUP)
        slabs = jnp.stack([ut[rows, kk * TOEP_CHUNK:(kk + 1) * TOEP_CHUNK] for kk in range(nk)])
        by_channel = jnp.swapaxes(slabs, 0, 1)
        for c in range(SSM_GROUP):
            o_ref[g, :, c * TOEP_CHUNK:(c + 1) * TOEP_CHUNK] = by_channel[c]


def _ut_call(x, norm_g, w_ut, tb):
    b, t, d = x.shape
    nk = tb // TOEP_CHUNK
    c2 = lambda i, j: (0, 0)
    return pl.pallas_call(
        _ut_kernel,
        grid=(b, t // tb),
        in_specs=[
            pl.BlockSpec((1, tb, d), lambda i, j: (i, j, 0)),
            pl.BlockSpec((1, d), c2),
            pl.BlockSpec((SSM_WIDTH, d), c2),
        ],
        out_specs=pl.BlockSpec((SSM_GROUPS, None, nk, TOEP_LANES), lambda i, j: (0, i, j, 0)),
        out_shape=jax.ShapeDtypeStruct((SSM_GROUPS, b, t // TOEP_CHUNK, TOEP_LANES), F32),
        compiler_params=pltpu.CompilerParams(
            dimension_semantics=("arbitrary", "arbitrary"), vmem_limit_bytes=VMEM_LIMIT_BYTES),
        name="ssm_input_proj_t",
    )(x, norm_g, w_ut)


def _powers(base_re, base_im, exponent, n_bits):
    pr = jnp.ones_like(base_re)
    pi = jnp.zeros_like(base_re)
    br, bi = base_re, base_im
    for bit in range(n_bits):
        take = ((exponent >> bit) & 1) == 1
        pr, pi = jnp.where(take, pr * br - pi * bi, pr), jnp.where(take, pr * bi + pi * br, pi)
        br, bi = br * br - bi * bi, 2.0 * br * bi
    return pr, pi


def _split_bf16(x):
    hi = x.astype(BF16)
    r = x - hi.astype(F32)
    mid = r.astype(BF16)
    lo = (r - mid.astype(F32)).astype(BF16)
    return hi, mid, lo


def _dot_f32(a, b):
    a3, b3 = _split_bf16(a), _split_bf16(b)
    acc = None
    for i in range(3):
        for j in range(3 - i):
            term = jnp.dot(a3[i], b3[j], preferred_element_type=F32)
            acc = term if acc is None else acc + term
    return acc


def _toeplitz_kernel(u_ref, h0_ref, lcr_ref, lci_ref, lrr_ref, lri_ref, cb_ref, bb1_ref, bb2_ref,
                     ct_ref, d_ref, y_ref, hfin_ref, a_ref, wb_ref, wc_ref, t_ref, ys_ref):
    nb, nk, _ = u_ref.shape
    rows = nb * nk
    L = TOEP_CHUNK
    n_bits = L.bit_length() - 1
    sub_i = lax.broadcasted_iota(jnp.int32, (L, L), 0)
    lane_i = lax.broadcasted_iota(jnp.int32, (L, L), 1)

    lcr, lci = lcr_ref[...], lci_ref[...]
    lane_p = lax.broadcasted_iota(jnp.int32, lcr.shape, 1)
    p0r, p0i = _powers(lcr, lci, lane_p, n_bits)
    p1r, p1i = p0r * lcr - p0i * lci, p0r * lci + p0i * lcr

    taps = _dot_f32(cb_ref[...], jnp.concatenate([p0r, -p0i], axis=0))

    lrr = jnp.broadcast_to(lrr_ref[...], (L, L))
    lri = jnp.broadcast_to(lri_ref[...], (L, L))
    qr, qi = _powers(lrr, lri, (L - 1) - sub_i, n_bits)
    for c in range(SSM_GROUP):
        wb_ref[c * L:(c + 1) * L, :] = (qr * bb1_ref[c:c + 1, :] + qi * bb2_ref[c:c + 1, :]).astype(BF16)
    pa = jnp.concatenate([p1r, p1i], axis=0)
    pb = jnp.concatenate([p1i, p1r], axis=0)
    for c in range(SSM_GROUP):
        wc_ref[:, c * L:(c + 1) * L] = (
            jnp.broadcast_to(ct_ref[:, c:c + 1], (L, L)) * pa
            + jnp.broadcast_to(ct_ref[:, SSM_GROUP + c:SSM_GROUP + c + 1], (L, L)) * pb).astype(BF16)

    a_ref[...] = u_ref[...].reshape(rows, TOEP_LANES).astype(BF16)

    s_in = jnp.dot(a_ref[...], wb_ref[...], preferred_element_type=F32)
    s_in = jnp.swapaxes(s_in.reshape(nb, nk, L), 0, 1)
    l128r = qr[0:1] * lrr[0:1] - qi[0:1] * lri[0:1]
    l128i = qr[0:1] * lri[0:1] + qi[0:1] * lrr[0:1]
    half = L // 2
    is_re = lane_i[0:1] < half
    mul_same = jnp.broadcast_to(l128r, (nb, L))
    mul_swap = jnp.broadcast_to(jnp.where(is_re, -l128i, l128i), (nb, L))
    s_sw = pltpu.roll(s_in.reshape(nk * nb, L), half, 1).reshape(nk, nb, L)
    h = h0_ref[...]
    h_sw = pltpu.roll(h, half, 1)
    h_prev = []
    for k in range(nk):
        h_prev.append(h)
        h, h_sw = (h * mul_same + h_sw * mul_swap + s_in[k],
                   h_sw * mul_same - h * mul_swap + s_sw[k])
    hfin_ref[...] = h
    h_prev = jnp.swapaxes(jnp.stack(h_prev), 0, 1).reshape(rows, L).astype(BF16)

    causal = lane_i >= sub_i
    for pair in range(SSM_GROUP // 2):
        slot = pair
        for ci in range(SSM_GROUP):
            for cc in range(2):
                r = ci * SSM_GROUP + 2 * pair + cc
                tap = jnp.broadcast_to(taps[r:r + 1, :], (L, L))
                toep = jnp.where(causal, pltpu.roll(tap, 0, 1, stride=1, stride_axis=0), 0.0)
                t_ref[slot, ci * L:(ci + 1) * L, cc * L:(cc + 1) * L] = toep.astype(BF16)
        cols = slice(2 * pair * L, (2 * pair + 2) * L)
        yv = jnp.dot(a_ref[...], t_ref[slot], preferred_element_type=F32)
        yv = yv + jnp.dot(h_prev, wc_ref[:, cols], preferred_element_type=F32)
        for cc in range(2):
            c = 2 * pair + cc
            uc = u_ref[:, :, c * L:(c + 1) * L]
            ys_ref[c] = yv[:, cc * L:(cc + 1) * L].reshape(nb, nk, L) + d_ref[c:c + 1, :] * uc
    for b in range(nb):
        y_ref[b] = jnp.swapaxes(ys_ref[:, b], 0, 1)


def _toeplitz_call(ut, h0, tabs):
    g, nb, nk, lanes = ut.shape
    L = TOEP_CHUNK
    rows = nb * nk
    lcr, lci, lrr, lri, cb, bb1, bb2, ct, dtab = tabs

    def per_group(*shape):
        return pl.BlockSpec((None,) + shape, lambda i: (i,) + (0,) * len(shape))

    return pl.pallas_call(
        _toeplitz_kernel,
        grid=(g,),
        in_specs=[
            per_group(nb, nk, lanes), per_group(nb, L),
            per_group(SSM_STATE, L), per_group(SSM_STATE, L), per_group(1, L), per_group(1, L),
            per_group(SSM_GROUP * SSM_GROUP, L), per_group(SSM_GROUP, L), per_group(SSM_GROUP, L),
            per_group(L, 2 * SSM_GROUP), per_group(SSM_GROUP, L),
        ],
        out_specs=[per_group(nb, nk, SSM_GROUP, L), per_group(nb, L)],
        out_shape=[
            jax.ShapeDtypeStruct((g, nb, nk, SSM_GROUP, L), F32),
            jax.ShapeDtypeStruct((g, nb, L), F32),
        ],
        scratch_shapes=[
            pltpu.VMEM((rows, lanes), BF16),
            pltpu.VMEM((lanes, L), BF16),
            pltpu.VMEM((L, lanes), BF16),
            pltpu.VMEM((SSM_GROUP // 2, lanes, 2 * L), BF16),
            pltpu.VMEM((SSM_GROUP, nb, nk, L), F32),
        ],
        compiler_params=pltpu.CompilerParams(
            dimension_semantics=("arbitrary",), vmem_limit_bytes=VMEM_LIMIT_BYTES),
        name="ssm_toeplitz",
    )(ut, h0, lcr, lci, lrr, lri, cb, bb1, bb2, ct, dtab)


def _layer_kernel(chunk, sub, y_by_group, x_ref, y_ref, cos_ref, sin_ref, s0_ref, ng_ref, w1_ref, rg_ref,
                  dm_ref, qd_ref, kd_ref, sd_ref, wglu_ref, bglu_ref, wo_ref, fg_ref,
                  out_ref, sfin_ref, o_ref):
    nb, tb, d = x_ref.shape
    rows = nb * tb

    @pl.when(pl.program_id(1) == 0)
    def _():
        sfin_ref[...] = s0_ref[...]

    def load_rows(ref, r0):
        if nb == 1:
            return ref[0, r0:r0 + sub, :]
        return ref[r0 // tb:(r0 + sub) // tb].reshape(sub, ref.shape[-1])

    def store_rows(ref, r0, val):
        if nb == 1:
            ref[0, r0:r0 + sub, :] = val
        else:
            ref[r0 // tb:(r0 + sub) // tb] = val.reshape(sub // tb, tb, ref.shape[-1])

    rg = rg_ref[...]
    for r0 in range(0, rows, sub):
        x = load_rows(x_ref, r0)
        hn = _rms(x, ng_ref[...]).astype(BF16)
        proj = jnp.dot(hn, w1_ref[...], preferred_element_type=F32)
        cos = cos_ref[r0:r0 + sub, :]
        sin = sin_ref[r0:r0 + sub, :]

        def rotary(a):
            return a * cos + pltpu.roll(a, HEAD_DIM // 2, 1) * sin

        for h in range(RET_HEADS):
            lanes = slice(h * HEAD_DIM, (h + 1) * HEAD_DIM)
            q = rotary(proj[:, lanes]).astype(BF16)
            k = rotary(proj[:, RET_WIDTH + h * HEAD_DIM:RET_WIDTH + (h + 1) * HEAD_DIM]) * (HEAD_DIM ** -0.5)
            kb = k.astype(BF16)
            vb = proj[:, 2 * RET_WIDTH + h * HEAD_DIM:2 * RET_WIDTH + (h + 1) * HEAD_DIM].astype(BF16)
            dm = dm_ref[h]
            qd = qd_ref[h]
            kd = kd_ref[h]
            sd = sd_ref[h]
            for c0 in range(0, sub, chunk):
                n = (r0 + c0) // tb
                r = slice(c0, c0 + chunk)
                qc, kc, vc = q[r], kb[r], vb[r]
                s = sfin_ref[n, h]
                scores = lax.dot_general(qc, kc, (((1,), (1,)), ((), ())), preferred_element_type=F32)
                p = (scores * dm).astype(BF16)
                o = jnp.dot(p, vc, preferred_element_type=F32)
                o = o + jnp.dot(qc, s.astype(BF16), preferred_element_type=F32) * qd
                kdt = (k[r] * kd).astype(BF16).T
                sfin_ref[n, h] = sd * s + jnp.dot(kdt, vc, preferred_element_type=F32)
                o_ref[r0 + c0:r0 + c0 + chunk, lanes] = o

        g_ret = proj[:, 3 * RET_WIDTH:4 * RET_WIDTH]
        g_ssm = proj[:, 4 * RET_WIDTH:4 * RET_WIDTH + SSM_WIDTH]
        o_parts = []
        for h in range(RET_HEADS):
            lanes = slice(h * HEAD_DIM, (h + 1) * HEAD_DIM)
            o_parts.append(_rms(o_ref[r0:r0 + sub, lanes], rg[:, lanes]))
        o = jnp.concatenate(o_parts, axis=-1) * jax.nn.silu(g_ret)

        if y_by_group:
            k0 = r0 // TOEP_CHUNK
            y = jnp.concatenate(
                [y_ref[:, k0 + j].reshape(SSM_WIDTH, TOEP_CHUNK).T for j in range(sub // TOEP_CHUNK)], axis=0)
        else:
            y = load_rows(y_ref, r0)
        z = jax.nn.gelu(y)
        gate = jnp.dot(z.astype(BF16), wglu_ref[...], preferred_element_type=F32) + bglu_ref[...]
        z = z * jax.nn.sigmoid(gate)
        z = z * jax.nn.silu(g_ssm)

        mix = jnp.concatenate([o, z], axis=-1).astype(BF16)
        res = x + jnp.dot(mix, wo_ref[...], preferred_element_type=F32)
        store_rows(out_ref, r0, _rms(res, fg_ref[...]))


def _layer_call(x, y, cos, sin, s0, norm_g, w1, ret_norm_g, dm, qd, kd, sd, w_glu, b_glu,
                w_out, final_g, plan):
    b, t, d = x.shape
    nb, tb, chunk, sub = plan.nb, plan.tb, plan.chunk, plan.sub
    rows = nb * tb
    c2 = lambda i, j: (0, 0)
    c3 = lambda i, j: (0, 0, 0)
    state_spec = pl.BlockSpec((nb, RET_HEADS, HEAD_DIM, HEAD_DIM), lambda i, j: (i, 0, 0, 0))
    if plan.toeplitz:
        y_spec = pl.BlockSpec((SSM_GROUPS, None, tb // TOEP_CHUNK, SSM_GROUP, TOEP_CHUNK),
                              lambda i, j: (0, i, j, 0, 0))
    else:
        y_spec = pl.BlockSpec((nb, tb, SSM_WIDTH), lambda i, j: (i, j, 0))
    return pl.pallas_call(
        functools.partial(_layer_kernel, chunk, sub, plan.toeplitz),
        grid=(b // nb, t // tb),
        in_specs=[
            pl.BlockSpec((nb, tb, d), lambda i, j: (i, j, 0)),
            y_spec,
            pl.BlockSpec((rows, HEAD_DIM), lambda i, j: (j, 0)),
            pl.BlockSpec((rows, HEAD_DIM), lambda i, j: (j, 0)),
            state_spec,
            pl.BlockSpec((1, d), c2),
            pl.BlockSpec(w1.shape, c2),
            pl.BlockSpec((1, RET_WIDTH), c2),
            pl.BlockSpec((RET_HEADS, chunk, chunk), c3),
            pl.BlockSpec((RET_HEADS, chunk, HEAD_DIM), c3),
            pl.BlockSpec((RET_HEADS, chunk, HEAD_DIM), c3),
            pl.BlockSpec((RET_HEADS, HEAD_DIM, HEAD_DIM), c3),
            pl.BlockSpec((SSM_WIDTH, SSM_WIDTH), c2),
            pl.BlockSpec((1, SSM_WIDTH), c2),
            pl.BlockSpec((D_MODEL, D_MODEL), c2),
            pl.BlockSpec((1, d), c2),
        ],
        out_specs=[
            pl.BlockSpec((nb, tb, d), lambda i, j: (i, j, 0)),
            state_spec,
        ],
        out_shape=[
            jax.ShapeDtypeStruct((b, t, d), F32),
            jax.ShapeDtypeStruct((b, RET_HEADS, HEAD_DIM, HEAD_DIM), F32),
        ],
        scratch_shapes=[pltpu.VMEM((rows, RET_WIDTH), F32)],
        compiler_params=pltpu.CompilerParams(
            dimension_semantics=("arbitrary", "arbitrary"), vmem_limit_bytes=VMEM_LIMIT_BYTES),
        name="retention_glu_layer",
    )(x, y, cos, sin, s0, norm_g, w1, ret_norm_g, dm, qd, kd, sd, w_glu, b_glu, w_out, final_g)


def _rotary_tables(pos, reps):
    half = HEAD_DIM // 2
    inv = ROPE_BASE ** (-jnp.arange(half, dtype=F32) / half)
    ang = pos.astype(F32)[:, None] * inv[None, :]
    cos, sin = jnp.cos(ang), jnp.sin(ang)
    cos2 = jnp.concatenate([cos, cos], axis=-1)
    sin2 = jnp.concatenate([-sin, sin], axis=-1)
    return jnp.tile(cos2, (reps, 1)), jnp.tile(sin2, (reps, 1))


def _retention_tables(chunk):
    log_g = jnp.log1p(-jnp.power(2.0, -5.0 - jnp.arange(RET_HEADS, dtype=F32)))
    idx = jnp.arange(chunk, dtype=F32)
    diff = idx[:, None] - idx[None, :]
    causal = diff >= 0
    dm = jnp.where(causal[None], jnp.exp(jnp.where(causal, diff, 0.0)[None] * log_g[:, None, None]), 0.0)
    q_decay = jnp.exp((idx + 1.0)[None, :] * log_g[:, None])
    k_decay = jnp.exp((chunk - 1.0 - idx)[None, :] * log_g[:, None])
    s_decay = jnp.exp(chunk * log_g)
    qd = jnp.broadcast_to(q_decay[:, :, None], (RET_HEADS, chunk, HEAD_DIM))
    kd = jnp.broadcast_to(k_decay[:, :, None], (RET_HEADS, chunk, HEAD_DIM))
    sd = jnp.broadcast_to(s_decay[:, None, None], (RET_HEADS, HEAD_DIM, HEAD_DIM))
    return dm, qd, kd, sd


def _discretise(lambda_re, lambda_im, log_step, b_re, b_im):
    lam = lax.complex(jnp.minimum(lambda_re.astype(F32), LAMBDA_RE_MAX), lambda_im.astype(F32))
    dt = jnp.exp(log_step.astype(F32))[:, None]
    lam_bar = jnp.exp(lam * dt)
    b_bar = ((lam_bar - 1.0) / lam)[..., None] * lax.complex(b_re.astype(F32), b_im.astype(F32))
    return lam_bar, b_bar


def _scan_tables(lam_bar, b_bar, c_re, c_im):
    gl = SSM_GROUPS // SSM_HALVES
    eye = jnp.eye(gl, dtype=F32)
    ar = jnp.real(lam_bar).reshape(SSM_HALVES, HALF_RI)
    ai = jnp.imag(lam_bar).reshape(SSM_HALVES, HALF_RI)

    def place_b(m):
        m = m.reshape(SSM_HALVES, gl, SSM_STATE, SSM_GROUP)
        return jnp.einsum("hgpc,gk->hgckp", m, eye).reshape(SSM_HALVES, HALF_IN, HALF_RI)

    def place_c(m):
        m = m.reshape(SSM_HALVES, gl, SSM_GROUP, SSM_STATE)
        return jnp.einsum("hgcp,gk->hkpgc", m, eye).reshape(SSM_HALVES, HALF_RI, HALF_IN)

    bd = jnp.concatenate([place_b(jnp.real(b_bar)), place_b(jnp.imag(b_bar))], axis=-1).astype(BF16)
    cd = jnp.concatenate([place_c(c_re.astype(F32)), place_c(-c_im.astype(F32))], axis=1).astype(BF16)
    return ar, ai, bd, cd


def _toeplitz_tables(lam_bar, b_bar, c_re, c_im, d):
    g, p = lam_bar.shape
    L = TOEP_CHUNK
    lre, lim = jnp.real(lam_bar), jnp.imag(lam_bar)
    bre, bim = jnp.real(b_bar), jnp.imag(b_bar)
    cre, cim = c_re.astype(F32), c_im.astype(F32)
    lcr = jnp.broadcast_to(lre[:, :, None], (g, p, L))
    lci = jnp.broadcast_to(lim[:, :, None], (g, p, L))
    lrr = jnp.concatenate([lre, lre], axis=-1)[:, None, :]
    lri = jnp.concatenate([lim, lim], axis=-1)[:, None, :]
    cb_re = cre[:, None, :, :] * bre.transpose(0, 2, 1)[:, :, None, :] - cim[:, None, :, :] * bim.transpose(0, 2, 1)[:, :, None, :]
    cb_im = cre[:, None, :, :] * bim.transpose(0, 2, 1)[:, :, None, :] + cim[:, None, :, :] * bre.transpose(0, 2, 1)[:, :, None, :]
    cb = jnp.concatenate([cb_re, cb_im], axis=-1).reshape(g, SSM_GROUP * SSM_GROUP, 2 * p)
    bt_re, bt_im = bre.transpose(0, 2, 1), bim.transpose(0, 2, 1)
    bb1 = jnp.concatenate([bt_re, bt_im], axis=-1)
    bb2 = jnp.concatenate([-bt_im, bt_re], axis=-1)
    ct_re, ct_im = cre.transpose(0, 2, 1), cim.transpose(0, 2, 1)
    col_a = jnp.concatenate([ct_re, -ct_re], axis=1)
    col_b = jnp.concatenate([-ct_im, -ct_im], axis=1)
    ct = jnp.concatenate([col_a, col_b], axis=-1)
    dtab = jnp.broadcast_to(d.astype(F32).reshape(g, SSM_GROUP, 1), (g, SSM_GROUP, L))
    return lcr, lci, lrr, lri, cb, bb1, bb2, ct, dtab


def _state_to_lanes(re, im):
    b = re.shape[0]
    re = re.astype(F32).reshape(b, SSM_HALVES, HALF_RI)
    im = im.astype(F32).reshape(b, SSM_HALVES, HALF_RI)
    return jnp.concatenate([re, im], axis=-1).reshape(b, N_STATE)


def _lanes_to_state(h):
    b = h.shape[0]
    h = h.reshape(b, SSM_HALVES, 2, HALF_RI)
    return (h[:, :, 0].reshape(b, SSM_GROUPS, SSM_STATE), h[:, :, 1].reshape(b, SSM_GROUPS, SSM_STATE))


class _Plan(NamedTuple):
    nb: int
    tb: int
    chunk: int
    sub: int
    toeplitz: bool
    ssm_tb: int


def _plan(b, t):
    if t >= LAYER_ROWS:
        return _Plan(nb=1, tb=LAYER_ROWS, chunk=LAYER_CHUNK, sub=LAYER_CHUNK, toeplitz=True,
                     ssm_tb=TOEP_PROJ_STEPS)
    nb = min(b, LAYER_ROWS // t)
    return _Plan(nb=nb, tb=t, chunk=t, sub=nb * t, toeplitz=False, ssm_tb=t)


def _stream_step(x, pos, ret_s0, h0_re, h0_im, weights):
    (norm_g, w1, w_u, ret_norm_g, lam_bar, b_bar, c_re, c_im, ssm_d, w_glu, b_glu, w_out, final_g) = weights
    b, t, _ = x.shape
    plan = _plan(b, t)
    if plan.toeplitz:
        ut = _ut_call(x, norm_g, w_u.T, plan.ssm_tb)
        h0 = jnp.concatenate([h0_re, h0_im], axis=-1).astype(F32).transpose(1, 0, 2)
        y, h_fin = _toeplitz_call(ut, h0, _toeplitz_tables(lam_bar, b_bar, c_re, c_im, ssm_d))
        h_fin = h_fin.transpose(1, 0, 2)
        h_re, h_im = h_fin[..., :SSM_STATE], h_fin[..., SSM_STATE:]
    else:
        ar, ai, bd, cd = _scan_tables(lam_bar, b_bar, c_re, c_im)
        y, h_fin = _ssm_call(x, norm_g, w_u, _state_to_lanes(h0_re, h0_im), ar, ai, bd, cd,
                             ssm_d[None, :].astype(F32), plan.ssm_tb)
        h_re, h_im = _lanes_to_state(h_fin)
    cos, sin = _rotary_tables(pos, plan.nb)
    dm, qd, kd, sd = _retention_tables(plan.chunk)
    out, s_fin = _layer_call(x, y, cos, sin, ret_s0, norm_g, w1, ret_norm_g, dm, qd, kd, sd,
                             w_glu, b_glu, w_out, final_g, plan)
    return out, s_fin, h_re, h_im


def kernel(x_prompt, x_sample, state_ret, state_ssm_re, state_ssm_im, norm_g, w_in, ret_norm_g,
           ssm_lambda_re, ssm_lambda_im, ssm_log_step, ssm_b_re, ssm_b_im, ssm_c_re, ssm_c_im,
           ssm_d, w_glu, b_glu, w_out, final_norm_g):
    depth = w_in.shape[0]
    assert depth == 1, "single-layer trunk"
    bp, tp, _ = x_prompt.shape
    bs, ts, _ = x_sample.shape
    l = 0
    w = w_in[l]
    u_lo = 4 * RET_WIDTH
    w1 = jnp.concatenate([w[:, :u_lo], w[:, u_lo + SSM_WIDTH:]], axis=1).astype(BF16)
    w_u = w[:, u_lo:u_lo + SSM_WIDTH].astype(BF16)
    lam_bar, b_bar = _discretise(ssm_lambda_re[l], ssm_lambda_im[l], ssm_log_step[l], ssm_b_re[l], ssm_b_im[l])
    weights = (
        norm_g[l][None, :].astype(F32), w1, w_u, ret_norm_g[l].reshape(1, RET_WIDTH).astype(F32),
        lam_bar, b_bar, ssm_c_re[l], ssm_c_im[l], ssm_d[l],
        w_glu[l].astype(BF16), b_glu[l][None, :].astype(F32),
        w_out[l].astype(BF16), final_norm_g[None, :].astype(F32),
    )

    pos_p = jnp.arange(tp, dtype=F32)
    pos_s = PAST_LEN + jnp.arange(ts, dtype=F32)
    ret0_p = jnp.zeros((bp, RET_HEADS, HEAD_DIM, HEAD_DIM), F32)
    h0_p = jnp.zeros((bp, SSM_GROUPS, SSM_STATE), F32)
    y_p, r_p, hp_re, hp_im = _stream_step(x_prompt, pos_p, ret0_p, h0_p, h0_p, weights)
    y_s, r_s, hs_re, hs_im = _stream_step(x_sample, pos_s, state_ret[l].astype(F32),
                                          state_ssm_re[l], state_ssm_im[l], weights)
    return (y_p, y_s, r_p[None], hp_re[None], hp_im[None], r_s[None], hs_re[None], hs_im[None])
```

```python
import functools
from typing import NamedTuple

import jax
import jax.numpy as jnp
import numpy as np
from jax import lax
from jax.experimental import pallas as pl
from jax.experimental.pallas import tpu as pltpu

D_MODEL = 1024
RET_HEADS = 4
HEAD_DIM = 128
RET_WIDTH = RET_HEADS * HEAD_DIM
SSM_WIDTH = 512
SSM_GROUP = 16
SSM_GROUPS = 32
SSM_STATE = 64
ROPE_BASE = 10000.0
EPS = 1e-6
LAMBDA_RE_MAX = -1e-4
PAST_LEN = 2048

LAYER_ROWS = 1024
LAYER_CHUNK = 256
TOEP_CHUNK = 128
TOEP_LANES = SSM_GROUP * TOEP_CHUNK
TOEP_PROJ_STEPS = 1024

SSM_HALVES = 2
HALF_IN = SSM_WIDTH // SSM_HALVES
HALF_RI = (SSM_GROUPS // SSM_HALVES) * SSM_STATE
HALF_STATE = 2 * HALF_RI
N_STATE = SSM_HALVES * HALF_STATE
SCAN_LANES = 512

VMEM_LIMIT_BYTES = 56 * 1024 * 1024

F32 = jnp.float32
BF16 = jnp.bfloat16


def _rms(x, g):
    return x * lax.rsqrt(jnp.mean(x * x, axis=-1, keepdims=True) + EPS) * g


def _ssm_kernel(x_ref, g_ref, wu_ref, h0_ref, ar_ref, ai_ref, bd_ref, cd_ref, d_ref, y_ref, hfin_ref,
                ubt_ref, utb_ref, bu_ref, hs_ref, ytb_ref):
    nb, tb, d_model = x_ref.shape
    w = SSM_WIDTH
    rows = nb * tb

    step = pl.program_id(0)

    @pl.when(step == 0)
    def _():
        ubt_ref[...] = jnp.zeros_like(ubt_ref)

    @pl.when(step <= 1)
    def _():
        hfin_ref[...] = h0_ref[...]

    for t in range(tb):
        utb_ref[t] = ubt_ref[:, t, :]
    u = utb_ref[...].reshape(rows, w)
    ub = u.astype(BF16)
    hn = _rms(x_ref[...].reshape(rows, d_model), g_ref[...]).astype(BF16)
    ubt_ref[...] = jnp.dot(hn, wu_ref[...], preferred_element_type=F32).reshape(nb, tb, w)

    for h in range(SSM_HALVES):
        cols = slice(h * HALF_IN, (h + 1) * HALF_IN)
        yh = d_ref[:, cols] * u[:, cols]
        for q in range(HALF_RI // SCAN_LANES):
            re_h = q * SCAN_LANES
            im_h = HALF_RI + re_h
            re = h * HALF_STATE + re_h
            im = h * HALF_STATE + im_h
            bu_ref[:, re:re + SCAN_LANES] = jnp.dot(
                ub[:, cols], bd_ref[h, :, re_h:re_h + SCAN_LANES], preferred_element_type=F32)
            bu_ref[:, im:im + SCAN_LANES] = jnp.dot(
                ub[:, cols], bd_ref[h, :, im_h:im_h + SCAN_LANES], preferred_element_type=F32)
            ar = jnp.broadcast_to(ar_ref[h:h + 1, re_h:re_h + SCAN_LANES], (nb, SCAN_LANES))
            ai = jnp.broadcast_to(ai_ref[h:h + 1, re_h:re_h + SCAN_LANES], (nb, SCAN_LANES))
            hr = hfin_ref[:, re:re + SCAN_LANES]
            hi = hfin_ref[:, im:im + SCAN_LANES]
            for t in range(tb):
                r = slice(t * nb, (t + 1) * nb)
                hr, hi = (ar * hr - ai * hi + bu_ref[r, re:re + SCAN_LANES],
                          ar * hi + ai * hr + bu_ref[r, im:im + SCAN_LANES])
                hs_ref[r, re:re + SCAN_LANES] = hr.astype(BF16)
                hs_ref[r, im:im + SCAN_LANES] = hi.astype(BF16)
            hfin_ref[:, re:re + SCAN_LANES] = hr
            hfin_ref[:, im:im + SCAN_LANES] = hi
            yh = yh + jnp.dot(hs_ref[:, re:re + SCAN_LANES], cd_ref[h, re_h:re_h + SCAN_LANES, :],
                              preferred_element_type=F32)
            yh = yh + jnp.dot(hs_ref[:, im:im + SCAN_LANES], cd_ref[h, im_h:im_h + SCAN_LANES, :],
                              preferred_element_type=F32)
        ytb_ref[:, :, cols] = yh.reshape(tb, nb, HALF_IN)
    for b in range(nb):
        y_ref[b] = ytb_ref[:, b, :]


def _ssm_call(x, norm_g, w_u, h0, ar, ai, bd, cd, d, tb):
    nb, t, d_model = x.shape
    w = SSM_WIDTH
    rows = nb * tb
    n_blocks = t // tb
    c2 = lambda i: (0, 0)
    c3 = lambda i: (0, 0, 0)
    return pl.pallas_call(
        _ssm_kernel,
        grid=(n_blocks + 1,),
        in_specs=[
            pl.BlockSpec((nb, tb, d_model), lambda i: (0, jnp.minimum(i, n_blocks - 1), 0)),
            pl.BlockSpec((1, d_model), c2),
            pl.BlockSpec((d_model, w), c2),
            pl.BlockSpec((nb, N_STATE), c2),
            pl.BlockSpec((SSM_HALVES, HALF_RI), c2),
            pl.BlockSpec((SSM_HALVES, HALF_RI), c2),
            pl.BlockSpec((SSM_HALVES, HALF_IN, HALF_STATE), c3),
            pl.BlockSpec((SSM_HALVES, HALF_STATE, HALF_IN), c3),
            pl.BlockSpec((1, w), c2),
        ],
        out_specs=[
            pl.BlockSpec((nb, tb, w), lambda i: (0, jnp.maximum(i - 1, 0), 0)),
            pl.BlockSpec((nb, N_STATE), c2),
        ],
        out_shape=[
            jax.ShapeDtypeStruct((nb, t, w), F32),
            jax.ShapeDtypeStruct((nb, N_STATE), F32),
        ],
        scratch_shapes=[
            pltpu.VMEM((nb, tb, w), F32),
            pltpu.VMEM((tb, nb, w), F32),
            pltpu.VMEM((rows, N_STATE), F32),
            pltpu.VMEM((rows, N_STATE), BF16),
            pltpu.VMEM((tb, nb, w), F32),
        ],
        compiler_params=pltpu.CompilerParams(
            dimension_semantics=("arbitrary",), vmem_limit_bytes=VMEM_LIMIT_BYTES),
        name="ssm_scan",
    )(x, norm_g, w_u, h0, ar, ai, bd, cd, d)


def _ut_kernel(x_ref, g_ref, wut_ref, o_ref):
    tb = x_ref.shape[1]
    nk = tb // TOEP_CHUNK
    hn = _rms(x_ref[0], g_ref[...]).astype(BF16)
    ut = lax.dot_general(wut_ref[...], hn, (((1,), (1,)), ((), ())), preferred_element_type=F32)
    for g in range(SSM_GROUPS):
        rows = slice(g * SSM_GROUP, (g + 1) * SSM_GROUP)
        slabs = jnp.stack([ut[rows, kk * TOEP_CHUNK:(kk + 1) * TOEP_CHUNK] for kk in range(nk)])
        by_channel = jnp.swapaxes(slabs, 0, 1)
        for c in range(SSM_GROUP):
            o_ref[g, :, c * TOEP_CHUNK:(c + 1) * TOEP_CHUNK] = by_channel[c]


def _ut_call(x, norm_g, w_ut, tb):
    b, t, d = x.shape
    nk = tb // TOEP_CHUNK
    c2 = lambda i, j: (0, 0)
    return pl.pallas_call(
        _ut_kernel,
        grid=(b, t // tb),
        in_specs=[
            pl.BlockSpec((1, tb, d), lambda i, j: (i, j, 0)),
            pl.BlockSpec((1, d), c2),
            pl.BlockSpec((SSM_WIDTH, d), c2),
        ],
        out_specs=pl.BlockSpec((SSM_GROUPS, None, nk, TOEP_LANES), lambda i, j: (0, i, j, 0)),
        out_shape=jax.ShapeDtypeStruct((SSM_GROUPS, b, t // TOEP_CHUNK, TOEP_LANES), F32),
        compiler_params=pltpu.CompilerParams(
            dimension_semantics=("arbitrary", "arbitrary"), vmem_limit_bytes=VMEM_LIMIT_BYTES),
        name="ssm_input_proj_t",
    )(x, norm_g, w_ut)


def _powers(base_re, base_im, exponent, n_bits):
    pr = jnp.ones_like(base_re)
    pi = jnp.zeros_like(base_re)
    br, bi = base_re, base_im
    for bit in range(n_bits):
        take = ((exponent >> bit) & 1) == 1
        pr, pi = jnp.where(take, pr * br - pi * bi, pr), jnp.where(take, pr * bi + pi * br, pi)
        br, bi = br * br - bi * bi, 2.0 * br * bi
    return pr, pi


def _split_bf16(x):
    hi = x.astype(BF16)
    r = x - hi.astype(F32)
    mid = r.astype(BF16)
    lo = (r - mid.astype(F32)).astype(BF16)
    return hi, mid, lo


def _dot_f32(a, b):
    a3, b3 = _split_bf16(a), _split_bf16(b)
    acc = None
    for i in range(3):
        for j in range(3 - i):
            term = jnp.dot(a3[i], b3[j], preferred_element_type=F32)
            acc = term if acc is None else acc + term
    return acc


def _toeplitz_kernel(u_ref, h0_ref, lcr_ref, lci_ref, lrr_ref, lri_ref, cb_ref, bb1_ref, bb2_ref,
                     ct_ref, d_ref, y_ref, hfin_ref, a_ref, wb_ref, wc_ref, t_ref, ys_ref):
    nb, nk, _ = u_ref.shape
    rows = nb * nk
    L = TOEP_CHUNK
    n_bits = L.bit_length() - 1
    sub_i = lax.broadcasted_iota(jnp.int32, (L, L), 0)
    lane_i = lax.broadcasted_iota(jnp.int32, (L, L), 1)

    lcr, lci = lcr_ref[...], lci_ref[...]
    lane_p = lax.broadcasted_iota(jnp.int32, lcr.shape, 1)
    p0r, p0i = _powers(lcr, lci, lane_p, n_bits)
    p1r, p1i = p0r * lcr - p0i * lci, p0r * lci + p0i * lcr

    taps = _dot_f32(cb_ref[...], jnp.concatenate([p0r, -p0i], axis=0))

    lrr = jnp.broadcast_to(lrr_ref[...], (L, L))
    lri = jnp.broadcast_to(lri_ref[...], (L, L))
    qr, qi = _powers(lrr, lri, (L - 1) - sub_i, n_bits)
    for c in range(SSM_GROUP):
        wb_ref[c * L:(c + 1) * L, :] = (qr * bb1_ref[c:c + 1, :] + qi * bb2_ref[c:c + 1, :]).astype(BF16)
    pa = jnp.concatenate([p1r, p1i], axis=0)
    pb = jnp.concatenate([p1i, p1r], axis=0)
    for c in range(SSM_GROUP):
        wc_ref[:, c * L:(c + 1) * L] = (
            jnp.broadcast_to(ct_ref[:, c:c + 1], (L, L)) * pa
            + jnp.broadcast_to(ct_ref[:, SSM_GROUP + c:SSM_GROUP + c + 1], (L, L)) * pb).astype(BF16)

    a_ref[...] = u_ref[...].reshape(rows, TOEP_LANES).astype(BF16)

    s_in = jnp.dot(a_ref[...], wb_ref[...], preferred_element_type=F32)
    s_in = jnp.swapaxes(s_in.reshape(nb, nk, L), 0, 1)
    l128r = qr[0:1] * lrr[0:1] - qi[0:1] * lri[0:1]
    l128i = qr[0:1] * lri[0:1] + qi[0:1] * lrr[0:1]
    half = L // 2
    is_re = lane_i[0:1] < half
    mul_same = jnp.broadcast_to(l128r, (nb, L))
    mul_swap = jnp.broadcast_to(jnp.where(is_re, -l128i, l128i), (nb, L))
    s_sw = pltpu.roll(s_in.reshape(nk * nb, L), half, 1).reshape(nk, nb, L)
    h = h0_ref[...]
    h_sw = pltpu.roll(h, half, 1)
    h_prev = []
    for k in range(nk):
        h_prev.append(h)
        h, h_sw = (h * mul_same + h_sw * mul_swap + s_in[k],
                   h_sw * mul_same - h * mul_swap + s_sw[k])
    hfin_ref[...] = h
    h_prev = jnp.swapaxes(jnp.stack(h_prev), 0, 1).reshape(rows, L).astype(BF16)

    causal = lane_i >= sub_i
    for pair in range(SSM_GROUP // 2):
        slot = pair
        for ci in range(SSM_GROUP):
            for cc in range(2):
                r = ci * SSM_GROUP + 2 * pair + cc
                tap = jnp.broadcast_to(taps[r:r + 1, :], (L, L))
                toep = jnp.where(causal, pltpu.roll(tap, 0, 1, stride=1, stride_axis=0), 0.0)
                t_ref[slot, ci * L:(ci + 1) * L, cc * L:(cc + 1) * L] = toep.astype(BF16)
        cols = slice(2 * pair * L, (2 * pair + 2) * L)
        yv = jnp.dot(a_ref[...], t_ref[slot], preferred_element_type=F32)
        yv = yv + jnp.dot(h_prev, wc_ref[:, cols], preferred_element_type=F32)
        for cc in range(2):
            c = 2 * pair + cc
            uc = u_ref[:, :, c * L:(c + 1) * L]
            ys_ref[c] = yv[:, cc * L:(cc + 1) * L].reshape(nb, nk, L) + d_ref[c:c + 1, :] * uc
    for b in range(nb):
        y_ref[b] = jnp.swapaxes(ys_ref[:, b], 0, 1)


def _toeplitz_call(ut, h0, tabs):
    g, nb, nk, lanes = ut.shape
    L = TOEP_CHUNK
    rows = nb * nk
    lcr, lci, lrr, lri, cb, bb1, bb2, ct, dtab = tabs

    def per_group(*shape):
        return pl.BlockSpec((None,) + shape, lambda i: (i,) + (0,) * len(shape))

    return pl.pallas_call(
        _toeplitz_kernel,
        grid=(g,),
        in_specs=[
            per_group(nb, nk, lanes), per_group(nb, L),
            per_group(SSM_STATE, L), per_group(SSM_STATE, L), per_group(1, L), per_group(1, L),
            per_group(SSM_GROUP * SSM_GROUP, L), per_group(SSM_GROUP, L), per_group(SSM_GROUP, L),
            per_group(L, 2 * SSM_GROUP), per_group(SSM_GROUP, L),
        ],
        out_specs=[per_group(nb, nk, SSM_GROUP, L), per_group(nb, L)],
        out_shape=[
            jax.ShapeDtypeStruct((g, nb, nk, SSM_GROUP, L), F32),
            jax.ShapeDtypeStruct((g, nb, L), F32),
        ],
        scratch_shapes=[
            pltpu.VMEM((rows, lanes), BF16),
            pltpu.VMEM((lanes, L), BF16),
            pltpu.VMEM((L, lanes), BF16),
            pltpu.VMEM((SSM_GROUP // 2, lanes, 2 * L), BF16),
            pltpu.VMEM((SSM_GROUP, nb, nk, L), F32),
        ],
        compiler_params=pltpu.CompilerParams(
            dimension_semantics=("arbitrary",), vmem_limit_bytes=VMEM_LIMIT_BYTES),
        name="ssm_toeplitz",
    )(ut, h0, lcr, lci, lrr, lri, cb, bb1, bb2, ct, dtab)


def _layer_kernel(chunk, sub, y_by_group, x_ref, y_ref, cos_ref, sin_ref, s0_ref, ng_ref, w1_ref, rg_ref,
                  dm_ref, qd_ref, kd_ref, sd_ref, wglu_ref, bglu_ref, wo_ref, fg_ref,
                  out_ref, sfin_ref, o_ref):
    nb, tb, d = x_ref.shape
    rows = nb * tb

    @pl.when(pl.program_id(1) == 0)
    def _():
        sfin_ref[...] = s0_ref[...]

    def load_rows(ref, r0):
        if nb == 1:
            return ref[0, r0:r0 + sub, :]
        return ref[r0 // tb:(r0 + sub) // tb].reshape(sub, ref.shape[-1])

    def store_rows(ref, r0, val):
        if nb == 1:
            ref[0, r0:r0 + sub, :] = val
        else:
            ref[r0 // tb:(r0 + sub) // tb] = val.reshape(sub // tb, tb, ref.shape[-1])

    rg = rg_ref[...]
    for r0 in range(0, rows, sub):
        x = load_rows(x_ref, r0)
        hn = _rms(x, ng_ref[...]).astype(BF16)
        proj = jnp.dot(hn, w1_ref[...], preferred_element_type=F32)
        cos = cos_ref[r0:r0 + sub, :]
        sin = sin_ref[r0:r0 + sub, :]

        def rotary(a):
            return a * cos + pltpu.roll(a, HEAD_DIM // 2, 1) * sin

        for h in range(RET_HEADS):
            lanes = slice(h * HEAD_DIM, (h + 1) * HEAD_DIM)
            q = rotary(proj[:, lanes]).astype(BF16)
            k = rotary(proj[:, RET_WIDTH + h * HEAD_DIM:RET_WIDTH + (h + 1) * HEAD_DIM]) * (HEAD_DIM ** -0.5)
            kb = k.astype(BF16)
            vb = proj[:, 2 * RET_WIDTH + h * HEAD_DIM:2 * RET_WIDTH + (h + 1) * HEAD_DIM].astype(BF16)
            dm = dm_ref[h]
            qd = qd_ref[h]
            kd = kd_ref[h]
            sd = sd_ref[h]
            for c0 in range(0, sub, chunk):
                n = (r0 + c0) // tb
                r = slice(c0, c0 + chunk)
                qc, kc, vc = q[r], kb[r], vb[r]
                s = sfin_ref[n, h]
                scores = lax.dot_general(qc, kc, (((1,), (1,)), ((), ())), preferred_element_type=F32)
                p = (scores * dm).astype(BF16)
                o = jnp.dot(p, vc, preferred_element_type=F32)
                o = o + jnp.dot(qc, s.astype(BF16), preferred_element_type=F32) * qd
                kdt = (k[r] * kd).astype(BF16).T
                sfin_ref[n, h] = sd * s + jnp.dot(kdt, vc, preferred_element_type=F32)
                o_ref[r0 + c0:r0 + c0 + chunk, lanes] = o

        g_ret = proj[:, 3 * RET_WIDTH:4 * RET_WIDTH]
        g_ssm = proj[:, 4 * RET_WIDTH:4 * RET_WIDTH + SSM_WIDTH]
        o_parts = []
        for h in range(RET_HEADS):
            lanes = slice(h * HEAD_DIM, (h + 1) * HEAD_DIM)
            o_parts.append(_rms(o_ref[r0:r0 + sub, lanes], rg[:, lanes]))
        o = jnp.concatenate(o_parts, axis=-1) * jax.nn.silu(g_ret)

        if y_by_group:
            k0 = r0 // TOEP_CHUNK
            y = jnp.concatenate(
                [y_ref[:, k0 + j].reshape(SSM_WIDTH, TOEP_CHUNK).T for j in range(sub // TOEP_CHUNK)], axis=0)
        else:
            y = load_rows(y_ref, r0)
        z = jax.nn.gelu(y)
        gate = jnp.dot(z.astype(BF16), wglu_ref[...], preferred_element_type=F32) + bglu_ref[...]
        z = z * jax.nn.sigmoid(gate)
        z = z * jax.nn.silu(g_ssm)

        mix = jnp.concatenate([o, z], axis=-1).astype(BF16)
        res = x + jnp.dot(mix, wo_ref[...], preferred_element_type=F32)
        store_rows(out_ref, r0, _rms(res, fg_ref[...]))


def _layer_call(x, y, cos, sin, s0, norm_g, w1, ret_norm_g, dm, qd, kd, sd, w_glu, b_glu,
                w_out, final_g, plan):
    b, t, d = x.shape
    nb, tb, chunk, sub = plan.nb, plan.tb, plan.chunk, plan.sub
    rows = nb * tb
    c2 = lambda i, j: (0, 0)
    c3 = lambda i, j: (0, 0, 0)
    state_spec = pl.BlockSpec((nb, RET_HEADS, HEAD_DIM, HEAD_DIM), lambda i, j: (i, 0, 0, 0))
    if plan.toeplitz:
        y_spec = pl.BlockSpec((SSM_GROUPS, None, tb // TOEP_CHUNK, SSM_GROUP, TOEP_CHUNK),
                              lambda i, j: (0, i, j, 0, 0))
    else:
        y_spec = pl.BlockSpec((nb, tb, SSM_WIDTH), lambda i, j: (i, j, 0))
    return pl.pallas_call(
        functools.partial(_layer_kernel, chunk, sub, plan.toeplitz),
        grid=(b // nb, t // tb),
        in_specs=[
            pl.BlockSpec((nb, tb, d), lambda i, j: (i, j, 0)),
            y_spec,
            pl.BlockSpec((rows, HEAD_DIM), lambda i, j: (j, 0)),
            pl.BlockSpec((rows, HEAD_DIM), lambda i, j: (j, 0)),
            state_spec,
            pl.BlockSpec((1, d), c2),
            pl.BlockSpec(w1.shape, c2),
            pl.BlockSpec((1, RET_WIDTH), c2),
            pl.BlockSpec((RET_HEADS, chunk, chunk), c3),
            pl.BlockSpec((RET_HEADS, chunk, HEAD_DIM), c3),
            pl.BlockSpec((RET_HEADS, chunk, HEAD_DIM), c3),
            pl.BlockSpec((RET_HEADS, HEAD_DIM, HEAD_DIM), c3),
            pl.BlockSpec((SSM_WIDTH, SSM_WIDTH), c2),
            pl.BlockSpec((1, SSM_WIDTH), c2),
            pl.BlockSpec((D_MODEL, D_MODEL), c2),
            pl.BlockSpec((1, d), c2),
        ],
        out_specs=[
            pl.BlockSpec((nb, tb, d), lambda i, j: (i, j, 0)),
            state_spec,
        ],
        out_shape=[
            jax.ShapeDtypeStruct((b, t, d), F32),
            jax.ShapeDtypeStruct((b, RET_HEADS, HEAD_DIM, HEAD_DIM), F32),
        ],
        scratch_shapes=[pltpu.VMEM((rows, RET_WIDTH), F32)],
        compiler_params=pltpu.CompilerParams(
            dimension_semantics=("arbitrary", "arbitrary"), vmem_limit_bytes=VMEM_LIMIT_BYTES),
        name="retention_glu_layer",
    )(x, y, cos, sin, s0, norm_g, w1, ret_norm_g, dm, qd, kd, sd, w_glu, b_glu, w_out, final_g)


def _rotary_tables(pos0, steps, reps):
    f = np.float32
    half = HEAD_DIM // 2
    inv = (f(ROPE_BASE) ** (-np.arange(half, dtype=f) / f(half))).astype(f)
    ang = (f(pos0) + np.arange(steps, dtype=f))[:, None] * inv[None, :]
    cos, sin = np.cos(ang), np.sin(ang)
    cos2 = np.concatenate([cos, cos], axis=-1)
    sin2 = np.concatenate([-sin, sin], axis=-1)
    return jnp.asarray(np.tile(cos2, (reps, 1)), F32), jnp.asarray(np.tile(sin2, (reps, 1)), F32)


def _retention_tables(chunk):
    f = np.float32
    log_g = np.log1p(-np.power(f(2.0), f(-5.0) - np.arange(RET_HEADS, dtype=f))).astype(f)
    idx = np.arange(chunk, dtype=f)
    diff = idx[:, None] - idx[None, :]
    causal = diff >= 0
    dm = np.where(causal[None], np.exp(np.where(causal, diff, f(0.0))[None] * log_g[:, None, None]), f(0.0))
    q_decay = np.exp((idx + f(1.0))[None, :] * log_g[:, None])
    k_decay = np.exp((f(chunk) - f(1.0) - idx)[None, :] * log_g[:, None])
    s_decay = np.exp(f(chunk) * log_g)
    qd = np.broadcast_to(q_decay[:, :, None], (RET_HEADS, chunk, HEAD_DIM))
    kd = np.broadcast_to(k_decay[:, :, None], (RET_HEADS, chunk, HEAD_DIM))
    sd = np.broadcast_to(s_decay[:, None, None], (RET_HEADS, HEAD_DIM, HEAD_DIM))
    return tuple(jnp.asarray(np.ascontiguousarray(a), F32) for a in (dm, qd, kd, sd))


def _discretise(lambda_re, lambda_im, log_step, b_re, b_im):
    lam = lax.complex(jnp.minimum(lambda_re.astype(F32), LAMBDA_RE_MAX), lambda_im.astype(F32))
    dt = jnp.exp(log_step.astype(F32))[:, None]
    lam_bar = jnp.exp(lam * dt)
    b_bar = ((lam_bar - 1.0) / lam)[..., None] * lax.complex(b_re.astype(F32), b_im.astype(F32))
    return lam_bar, b_bar


def _scan_tables(lam_bar, b_bar, c_re, c_im):
    gl = SSM_GROUPS // SSM_HALVES
    eye = jnp.eye(gl, dtype=F32)
    ar = jnp.real(lam_bar).reshape(SSM_HALVES, HALF_RI)
    ai = jnp.imag(lam_bar).reshape(SSM_HALVES, HALF_RI)

    def place_b(m):
        m = m.reshape(SSM_HALVES, gl, SSM_STATE, SSM_GROUP)
        return jnp.einsum("hgpc,gk->hgckp", m, eye).reshape(SSM_HALVES, HALF_IN, HALF_RI)

    def place_c(m):
        m = m.reshape(SSM_HALVES, gl, SSM_GROUP, SSM_STATE)
        return jnp.einsum("hgcp,gk->hkpgc", m, eye).reshape(SSM_HALVES, HALF_RI, HALF_IN)

    bd = jnp.concatenate([place_b(jnp.real(b_bar)), place_b(jnp.imag(b_bar))], axis=-1).astype(BF16)
    cd = jnp.concatenate([place_c(c_re.astype(F32)), place_c(-c_im.astype(F32))], axis=1).astype(BF16)
    return ar, ai, bd, cd


def _toeplitz_tables(lam_bar, b_bar, c_re, c_im, d):
    g, p = lam_bar.shape
    L = TOEP_CHUNK
    lre, lim = jnp.real(lam_bar), jnp.imag(lam_bar)
    bre, bim = jnp.real(b_bar), jnp.imag(b_bar)
    cre, cim = c_re.astype(F32), c_im.astype(F32)
    lcr = jnp.broadcast_to(lre[:, :, None], (g, p, L))
    lci = jnp.broadcast_to(lim[:, :, None], (g, p, L))
    lrr = jnp.concatenate([lre, lre], axis=-1)[:, None, :]
    lri = jnp.concatenate([lim, lim], axis=-1)[:, None, :]
    cb_re = cre[:, None, :, :] * bre.transpose(0, 2, 1)[:, :, None, :] - cim[:, None, :, :] * bim.transpose(0, 2, 1)[:, :, None, :]
    cb_im = cre[:, None, :, :] * bim.transpose(0, 2, 1)[:, :, None, :] + cim[:, None, :, :] * bre.transpose(0, 2, 1)[:, :, None, :]
    cb = jnp.concatenate([cb_re, cb_im], axis=-1).reshape(g, SSM_GROUP * SSM_GROUP, 2 * p)
    bt_re, bt_im = bre.transpose(0, 2, 1), bim.transpose(0, 2, 1)
    bb1 = jnp.concatenate([bt_re, bt_im], axis=-1)
    bb2 = jnp.concatenate([-bt_im, bt_re], axis=-1)
    ct_re, ct_im = cre.transpose(0, 2, 1), cim.transpose(0, 2, 1)
    col_a = jnp.concatenate([ct_re, -ct_re], axis=1)
    col_b = jnp.concatenate([-ct_im, -ct_im], axis=1)
    ct = jnp.concatenate([col_a, col_b], axis=-1)
    dtab = jnp.broadcast_to(d.astype(F32).reshape(g, SSM_GROUP, 1), (g, SSM_GROUP, L))
    return lcr, lci, lrr, lri, cb, bb1, bb2, ct, dtab


def _state_to_lanes(re, im):
    b = re.shape[0]
    re = re.astype(F32).reshape(b, SSM_HALVES, HALF_RI)
    im = im.astype(F32).reshape(b, SSM_HALVES, HALF_RI)
    return jnp.concatenate([re, im], axis=-1).reshape(b, N_STATE)


def _lanes_to_state(h):
    b = h.shape[0]
    h = h.reshape(b, SSM_HALVES, 2, HALF_RI)
    return (h[:, :, 0].reshape(b, SSM_GROUPS, SSM_STATE), h[:, :, 1].reshape(b, SSM_GROUPS, SSM_STATE))


class _Plan(NamedTuple):
    nb: int
    tb: int
    chunk: int
    sub: int
    toeplitz: bool
    ssm_tb: int


def _plan(b, t):
    if t >= LAYER_ROWS:
        return _Plan(nb=1, tb=LAYER_ROWS, chunk=LAYER_CHUNK, sub=LAYER_CHUNK, toeplitz=True,
                     ssm_tb=TOEP_PROJ_STEPS)
    nb = min(b, LAYER_ROWS // t)
    return _Plan(nb=nb, tb=t, chunk=t, sub=nb * t, toeplitz=False, ssm_tb=t)


def _stream_step(x, pos0, ret_s0, h0_re, h0_im, weights):
    (norm_g, w1, w_u, ret_norm_g, lam_bar, b_bar, c_re, c_im, ssm_d, w_glu, b_glu, w_out, final_g) = weights
    b, t, _ = x.shape
    plan = _plan(b, t)
    if plan.toeplitz:
        ut = _ut_call(x, norm_g, w_u.T, plan.ssm_tb)
        h0 = jnp.concatenate([h0_re, h0_im], axis=-1).astype(F32).transpose(1, 0, 2)
        y, h_fin = _toeplitz_call(ut, h0, _toeplitz_tables(lam_bar, b_bar, c_re, c_im, ssm_d))
        h_fin = h_fin.transpose(1, 0, 2)
        h_re, h_im = h_fin[..., :SSM_STATE], h_fin[..., SSM_STATE:]
    else:
        ar, ai, bd, cd = _scan_tables(lam_bar, b_bar, c_re, c_im)
        y, h_fin = _ssm_call(x, norm_g, w_u, _state_to_lanes(h0_re, h0_im), ar, ai, bd, cd,
                             ssm_d[None, :].astype(F32), plan.ssm_tb)
        h_re, h_im = _lanes_to_state(h_fin)
    cos, sin = _rotary_tables(pos0, t, plan.nb)
    dm, qd, kd, sd = _retention_tables(plan.chunk)
    out, s_fin = _layer_call(x, y, cos, sin, ret_s0, norm_g, w1, ret_norm_g, dm, qd, kd, sd,
                             w_glu, b_glu, w_out, final_g, plan)
    return out, s_fin, h_re, h_im


def kernel(x_prompt, x_sample, state_ret, state_ssm_re, state_ssm_im, norm_g, w_in, ret_norm_g,
           ssm_lambda_re, ssm_lambda_im, ssm_log_step, ssm_b_re, ssm_b_im, ssm_c_re, ssm_c_im,
           ssm_d, w_glu, b_glu, w_out, final_norm_g):
    depth = w_in.shape[0]
    assert depth == 1, "single-layer trunk"
    bp, tp, _ = x_prompt.shape
    bs, ts, _ = x_sample.shape
    l = 0
    w = w_in[l]
    u_lo = 4 * RET_WIDTH
    w1 = jnp.concatenate([w[:, :u_lo], w[:, u_lo + SSM_WIDTH:]], axis=1).astype(BF16)
    w_u = w[:, u_lo:u_lo + SSM_WIDTH].astype(BF16)
    lam_bar, b_bar = _discretise(ssm_lambda_re[l], ssm_lambda_im[l], ssm_log_step[l], ssm_b_re[l], ssm_b_im[l])
    weights = (
        norm_g[l][None, :].astype(F32), w1, w_u, ret_norm_g[l].reshape(1, RET_WIDTH).astype(F32),
        lam_bar, b_bar, ssm_c_re[l], ssm_c_im[l], ssm_d[l],
        w_glu[l].astype(BF16), b_glu[l][None, :].astype(F32),
        w_out[l].astype(BF16), final_norm_g[None, :].astype(F32),
    )

    ret0_p = jnp.zeros((bp, RET_HEADS, HEAD_DIM, HEAD_DIM), F32)
    h0_p = jnp.zeros((bp, SSM_GROUPS, SSM_STATE), F32)
    y_p, r_p, hp_re, hp_im = _stream_step(x_prompt, 0, ret0_p, h0_p, h0_p, weights)
    y_s, r_s, hs_re, hs_im = _stream_step(x_sample, PAST_LEN, state_ret[l].astype(F32),
                                          state_ssm_re[l], state_ssm_im[l], weights)
    return (y_p, y_s, r_p[None], hp_re[None], hp_im[None], r_s[None], hs_re[None], hs_im[None])
```

```python
import functools
from typing import NamedTuple

import jax
import jax.numpy as jnp
import numpy as np
from jax import lax
from jax.experimental import pallas as pl
from jax.experimental.pallas import tpu as pltpu

D_MODEL = 1024
RET_HEADS = 4
HEAD_DIM = 128
RET_WIDTH = RET_HEADS * HEAD_DIM
SSM_WIDTH = 512
SSM_GROUP = 16
SSM_GROUPS = 32
SSM_STATE = 64
ROPE_BASE = 10000.0
EPS = 1e-6
LAMBDA_RE_MAX = -1e-4
PAST_LEN = 2048

LAYER_ROWS = 1024
LAYER_CHUNK = 256
TOEP_CHUNK = 128
TOEP_LANES = SSM_GROUP * TOEP_CHUNK
TOEP_PROJ_STEPS = 1024

SSM_HALVES = 2
HALF_IN = SSM_WIDTH // SSM_HALVES
HALF_RI = (SSM_GROUPS // SSM_HALVES) * SSM_STATE
HALF_STATE = 2 * HALF_RI
N_STATE = SSM_HALVES * HALF_STATE
SCAN_LANES = 512

VMEM_LIMIT_BYTES = 56 * 1024 * 1024

F32 = jnp.float32
BF16 = jnp.bfloat16


def _rms(x, g):
    return x * lax.rsqrt(jnp.mean(x * x, axis=-1, keepdims=True) + EPS) * g


def _ssm_kernel(x_ref, g_ref, wu_ref, h0_ref, ar_ref, ai_ref, bd_ref, cd_ref, d_ref, y_ref, hfin_ref,
                ubt_ref, utb_ref, bu_ref, hs_ref, ytb_ref):
    nb, tb, d_model = x_ref.shape
    w = SSM_WIDTH
    rows = nb * tb

    @pl.when(pl.program_id(0) == 0)
    def _():
        hfin_ref[...] = h0_ref[...]

    hn = _rms(x_ref[...].reshape(rows, d_model), g_ref[...]).astype(BF16)
    ubt_ref[...] = jnp.dot(hn, wu_ref[...], preferred_element_type=F32).reshape(nb, tb, w)
    for t in range(tb):
        utb_ref[t] = ubt_ref[:, t, :]
    u = utb_ref[...].reshape(rows, w)
    ub = u.astype(BF16)

    for h in range(SSM_HALVES):
        cols = slice(h * HALF_IN, (h + 1) * HALF_IN)
        yh = d_ref[:, cols] * u[:, cols]
        for q in range(HALF_RI // SCAN_LANES):
            re_h = q * SCAN_LANES
            im_h = HALF_RI + re_h
            re = h * HALF_STATE + re_h
            im = h * HALF_STATE + im_h
            bu_ref[:, re:re + SCAN_LANES] = jnp.dot(
                ub[:, cols], bd_ref[h, :, re_h:re_h + SCAN_LANES], preferred_element_type=F32)
            bu_ref[:, im:im + SCAN_LANES] = jnp.dot(
                ub[:, cols], bd_ref[h, :, im_h:im_h + SCAN_LANES], preferred_element_type=F32)
            ar = jnp.broadcast_to(ar_ref[h:h + 1, re_h:re_h + SCAN_LANES], (nb, SCAN_LANES))
            ai = jnp.broadcast_to(ai_ref[h:h + 1, re_h:re_h + SCAN_LANES], (nb, SCAN_LANES))
            hr = hfin_ref[:, re:re + SCAN_LANES]
            hi = hfin_ref[:, im:im + SCAN_LANES]
            for t in range(tb):
                r = slice(t * nb, (t + 1) * nb)
                hr, hi = (ar * hr - ai * hi + bu_ref[r, re:re + SCAN_LANES],
                          ar * hi + ai * hr + bu_ref[r, im:im + SCAN_LANES])
                hs_ref[r, re:re + SCAN_LANES] = hr.astype(BF16)
                hs_ref[r, im:im + SCAN_LANES] = hi.astype(BF16)
            hfin_ref[:, re:re + SCAN_LANES] = hr
            hfin_ref[:, im:im + SCAN_LANES] = hi
            yh = yh + jnp.dot(hs_ref[:, re:re + SCAN_LANES], cd_ref[h, re_h:re_h + SCAN_LANES, :],
                              preferred_element_type=F32)
            yh = yh + jnp.dot(hs_ref[:, im:im + SCAN_LANES], cd_ref[h, im_h:im_h + SCAN_LANES, :],
                              preferred_element_type=F32)
        ytb_ref[:, :, cols] = yh.reshape(tb, nb, HALF_IN)
    for b in range(nb):
        y_ref[b] = ytb_ref[:, b, :]


def _ssm_call(x, norm_g, w_u, h0, ar, ai, bd, cd, d, tb):
    nb, t, d_model = x.shape
    w = SSM_WIDTH
    rows = nb * tb
    n_blocks = t // tb
    c2 = lambda i: (0, 0)
    c3 = lambda i: (0, 0, 0)
    return pl.pallas_call(
        _ssm_kernel,
        grid=(n_blocks,),
        in_specs=[
            pl.BlockSpec((nb, tb, d_model), lambda i: (0, i, 0)),
            pl.BlockSpec((1, d_model), c2),
            pl.BlockSpec((d_model, w), c2),
            pl.BlockSpec((nb, N_STATE), c2),
            pl.BlockSpec((SSM_HALVES, HALF_RI), c2),
            pl.BlockSpec((SSM_HALVES, HALF_RI), c2),
            pl.BlockSpec((SSM_HALVES, HALF_IN, HALF_STATE), c3),
            pl.BlockSpec((SSM_HALVES, HALF_STATE, HALF_IN), c3),
            pl.BlockSpec((1, w), c2),
        ],
        out_specs=[
            pl.BlockSpec((nb, tb, w), lambda i: (0, i, 0)),
            pl.BlockSpec((nb, N_STATE), c2),
        ],
        out_shape=[
            jax.ShapeDtypeStruct((nb, t, w), F32),
            jax.ShapeDtypeStruct((nb, N_STATE), F32),
        ],
        scratch_shapes=[
            pltpu.VMEM((nb, tb, w), F32),
            pltpu.VMEM((tb, nb, w), F32),
            pltpu.VMEM((rows, N_STATE), F32),
            pltpu.VMEM((rows, N_STATE), BF16),
            pltpu.VMEM((tb, nb, w), F32),
        ],
        compiler_params=pltpu.CompilerParams(
            dimension_semantics=("arbitrary",), vmem_limit_bytes=VMEM_LIMIT_BYTES),
        name="ssm_scan",
    )(x, norm_g, w_u, h0, ar, ai, bd, cd, d)


def _front_kernel(emit_ut, sub, x_ref, g_ref, *refs):
    if emit_ut:
        wut_ref, wqkv_ref, cos_ref, sin_ref, kd_ref, ut_ref, q_ref, k_ref, kdk_ref, v_ref = refs
    else:
        wqkv_ref, cos_ref, sin_ref, kd_ref, q_ref, k_ref, kdk_ref, v_ref = refs
    nb, tb, d = x_ref.shape
    rows = nb * tb

    def load_rows(ref, r0):
        if nb == 1:
            return ref[0, r0:r0 + sub, :]
        return ref[r0 // tb:(r0 + sub) // tb].reshape(sub, ref.shape[-1])

    def store_rows(ref, r0, lanes, val):
        if nb == 1:
            ref[0, r0:r0 + sub, lanes] = val
        else:
            ref[r0 // tb:(r0 + sub) // tb, :, lanes] = val.reshape(sub // tb, tb, val.shape[-1])

    hns = []
    for r0 in range(0, rows, sub):
        hn = _rms(load_rows(x_ref, r0), g_ref[...]).astype(BF16)
        hns.append(hn)
        qkv = jnp.dot(hn, wqkv_ref[...], preferred_element_type=F32)
        cos = cos_ref[r0:r0 + sub, :]
        sin = sin_ref[r0:r0 + sub, :]

        def rotary(a):
            return a * cos + pltpu.roll(a, HEAD_DIM // 2, 1) * sin

        for h in range(RET_HEADS):
            lanes = slice(h * HEAD_DIM, (h + 1) * HEAD_DIM)
            k = rotary(qkv[:, RET_WIDTH + h * HEAD_DIM:RET_WIDTH + (h + 1) * HEAD_DIM]) * (HEAD_DIM ** -0.5)
            store_rows(q_ref, r0, lanes, rotary(qkv[:, lanes]).astype(BF16))
            store_rows(k_ref, r0, lanes, k.astype(BF16))
            store_rows(kdk_ref, r0, lanes, (k * kd_ref[:, lanes]).astype(BF16))
            store_rows(v_ref, r0, lanes,
                       qkv[:, 2 * RET_WIDTH + h * HEAD_DIM:2 * RET_WIDTH + (h + 1) * HEAD_DIM].astype(BF16))

    if emit_ut:
        nk = rows // TOEP_CHUNK
        hn = jnp.concatenate(hns, axis=0)
        ut = lax.dot_general(wut_ref[...], hn, (((1,), (1,)), ((), ())), preferred_element_type=F32)
        for g in range(SSM_GROUPS):
            grp = slice(g * SSM_GROUP, (g + 1) * SSM_GROUP)
            slabs = jnp.stack([ut[grp, kk * TOEP_CHUNK:(kk + 1) * TOEP_CHUNK] for kk in range(nk)])
            by_channel = jnp.swapaxes(slabs, 0, 1)
            for c in range(SSM_GROUP):
                ut_ref[g, :, c * TOEP_CHUNK:(c + 1) * TOEP_CHUNK] = by_channel[c]


def _front_call(x, norm_g, w_ut, w_qkv, cos, sin, kd_rows, nb, tb, sub, emit_ut):
    b, t, d = x.shape
    rows = nb * tb
    c2 = lambda i, j: (0, 0)
    row_spec = pl.BlockSpec((nb, tb, RET_WIDTH), lambda i, j: (i, j, 0))
    in_specs = [pl.BlockSpec((nb, tb, d), lambda i, j: (i, j, 0)), pl.BlockSpec((1, d), c2)]
    operands = [x, norm_g]
    out_specs, out_shape = [], []
    if emit_ut:
        in_specs.append(pl.BlockSpec((SSM_WIDTH, d), c2))
        operands.append(w_ut)
        out_specs.append(pl.BlockSpec((SSM_GROUPS, None, tb // TOEP_CHUNK, TOEP_LANES), lambda i, j: (0, i, j, 0)))
        out_shape.append(jax.ShapeDtypeStruct((SSM_GROUPS, b, t // TOEP_CHUNK, TOEP_LANES), F32))
    in_specs += [
        pl.BlockSpec((d, 3 * RET_WIDTH), c2),
        pl.BlockSpec((rows, HEAD_DIM), lambda i, j: (j, 0)),
        pl.BlockSpec((rows, HEAD_DIM), lambda i, j: (j, 0)),
        pl.BlockSpec((sub, RET_WIDTH), c2),
    ]
    operands += [w_qkv, cos, sin, kd_rows]
    out_specs += [row_spec] * 4
    out_shape += [jax.ShapeDtypeStruct((b, t, RET_WIDTH), BF16)] * 4
    return pl.pallas_call(
        functools.partial(_front_kernel, emit_ut, sub),
        grid=(b // nb, t // tb),
        in_specs=in_specs,
        out_specs=out_specs,
        out_shape=out_shape,
        compiler_params=pltpu.CompilerParams(
            dimension_semantics=("arbitrary", "arbitrary"), vmem_limit_bytes=VMEM_LIMIT_BYTES),
        name="qkv_rotary_front",
    )(*operands)


def _powers(base_re, base_im, exponent, n_bits):
    pr = jnp.ones_like(base_re)
    pi = jnp.zeros_like(base_re)
    br, bi = base_re, base_im
    for bit in range(n_bits):
        take = ((exponent >> bit) & 1) == 1
        pr, pi = jnp.where(take, pr * br - pi * bi, pr), jnp.where(take, pr * bi + pi * br, pi)
        br, bi = br * br - bi * bi, 2.0 * br * bi
    return pr, pi


def _split_bf16(x):
    hi = x.astype(BF16)
    r = x - hi.astype(F32)
    mid = r.astype(BF16)
    lo = (r - mid.astype(F32)).astype(BF16)
    return hi, mid, lo


def _dot_f32(a, b):
    a3, b3 = _split_bf16(a), _split_bf16(b)
    acc = None
    for i in range(3):
        for j in range(3 - i):
            term = jnp.dot(a3[i], b3[j], preferred_element_type=F32)
            acc = term if acc is None else acc + term
    return acc


_TOEP_TAB_ROWS = ((0, 64), (64, 64), (128, 256), (384, 16), (400, 16), (416, 16), (432, 1), (440, 1), (448, 128))
_TOEP_TAB_HEIGHT = 576


def _toeplitz_kernel(u_ref, h0_ref, tab_ref, y_ref, hfin_ref, a_ref, wb_ref, wc_ref, t_ref, ys_ref, yi_ref):
    lcr_ref, lci_ref, cb_ref, bb1_ref, bb2_ref, d_ref, lrr_ref, lri_ref, ct_ref = (
        tab_ref.at[pl.ds(lo, n)] for lo, n in _TOEP_TAB_ROWS)
    nb, nk, _ = u_ref.shape
    rows = nb * nk
    L = TOEP_CHUNK
    n_bits = L.bit_length() - 1
    sub_i = lax.broadcasted_iota(jnp.int32, (L, L), 0)
    lane_i = lax.broadcasted_iota(jnp.int32, (L, L), 1)

    lcr, lci = lcr_ref[...], lci_ref[...]
    lane_p = lax.broadcasted_iota(jnp.int32, lcr.shape, 1)
    p0r, p0i = _powers(lcr, lci, lane_p, n_bits)
    p1r, p1i = p0r * lcr - p0i * lci, p0r * lci + p0i * lcr

    taps = _dot_f32(cb_ref[...], jnp.concatenate([p0r, -p0i], axis=0))

    lrr = jnp.broadcast_to(lrr_ref[...], (L, L))
    lri = jnp.broadcast_to(lri_ref[...], (L, L))
    qr, qi = _powers(lrr, lri, (L - 1) - sub_i, n_bits)
    for c in range(SSM_GROUP):
        wb_ref[c * L:(c + 1) * L, :] = (qr * bb1_ref[c:c + 1, :] + qi * bb2_ref[c:c + 1, :]).astype(BF16)
    pa = jnp.concatenate([p1r, p1i], axis=0)
    pb = jnp.concatenate([p1i, p1r], axis=0)
    for c in range(SSM_GROUP):
        wc_ref[:, c * L:(c + 1) * L] = (
            jnp.broadcast_to(ct_ref[:, c:c + 1], (L, L)) * pa
            + jnp.broadcast_to(ct_ref[:, SSM_GROUP + c:SSM_GROUP + c + 1], (L, L)) * pb).astype(BF16)

    a_ref[...] = u_ref[...].reshape(rows, TOEP_LANES).astype(BF16)

    s_in = jnp.dot(a_ref[...], wb_ref[...], preferred_element_type=F32)
    s_in = jnp.swapaxes(s_in.reshape(nb, nk, L), 0, 1)
    l128r = qr[0:1] * lrr[0:1] - qi[0:1] * lri[0:1]
    l128i = qr[0:1] * lri[0:1] + qi[0:1] * lrr[0:1]
    half = L // 2
    is_re = lane_i[0:1] < half
    mul_same = jnp.broadcast_to(l128r, (nb, L))
    mul_swap = jnp.broadcast_to(jnp.where(is_re, -l128i, l128i), (nb, L))
    s_sw = pltpu.roll(s_in.reshape(nk * nb, L), half, 1).reshape(nk, nb, L)
    h = h0_ref[...]
    h_sw = pltpu.roll(h, half, 1)
    h_prev = []
    for k in range(nk):
        h_prev.append(h)
        h, h_sw = (h * mul_same + h_sw * mul_swap + s_in[k],
                   h_sw * mul_same - h * mul_swap + s_sw[k])
    hfin_ref[...] = h
    h_prev = jnp.swapaxes(jnp.stack(h_prev), 0, 1).reshape(rows, L).astype(BF16)

    causal = lane_i >= sub_i
    for pair in range(SSM_GROUP // 2):
        slot = pair
        for ci in range(SSM_GROUP):
            for cc in range(2):
                r = ci * SSM_GROUP + 2 * pair + cc
                tap = jnp.broadcast_to(taps[r:r + 1, :], (L, L))
                toep = jnp.where(causal, pltpu.roll(tap, 0, 1, stride=1, stride_axis=0), 0.0)
                t_ref[slot, ci * L:(ci + 1) * L, cc * L:(cc + 1) * L] = toep.astype(BF16)
        yv = jnp.dot(a_ref[...], t_ref[slot], preferred_element_type=F32)
        for cc in range(2):
            c = 2 * pair + cc
            uc = u_ref[:, :, c * L:(c + 1) * L]
            ys_ref[c] = yv[:, cc * L:(cc + 1) * L].reshape(nb, nk, L) + d_ref[c:c + 1, :] * uc
    inter = jnp.dot(h_prev, wc_ref[...], preferred_element_type=F32)
    for c in range(SSM_GROUP):
        yi_ref[c] = inter[:, c * L:(c + 1) * L].reshape(nb, nk, L)
    for b in range(nb):
        y_ref[b] = jnp.swapaxes(ys_ref[:, b] + yi_ref[:, b], 0, 1)


def _toeplitz_call(ut, h0, tab):
    g, nb, nk, lanes = ut.shape
    L = TOEP_CHUNK
    rows = nb * nk

    def per_group(*shape):
        return pl.BlockSpec((None,) + shape, lambda i: (i,) + (0,) * len(shape))

    return pl.pallas_call(
        _toeplitz_kernel,
        grid=(g,),
        in_specs=[per_group(nb, nk, lanes), per_group(nb, L), per_group(_TOEP_TAB_HEIGHT, L)],
        out_specs=[per_group(nb, nk, SSM_GROUP, L), per_group(nb, L)],
        out_shape=[
            jax.ShapeDtypeStruct((g, nb, nk, SSM_GROUP, L), F32),
            jax.ShapeDtypeStruct((g, nb, L), F32),
        ],
        scratch_shapes=[
            pltpu.VMEM((rows, lanes), BF16),
            pltpu.VMEM((lanes, L), BF16),
            pltpu.VMEM((L, lanes), BF16),
            pltpu.VMEM((SSM_GROUP // 2, lanes, 2 * L), BF16),
            pltpu.VMEM((SSM_GROUP, nb, nk, L), F32),
            pltpu.VMEM((SSM_GROUP, nb, nk, L), F32),
        ],
        compiler_params=pltpu.CompilerParams(
            dimension_semantics=("arbitrary",), vmem_limit_bytes=VMEM_LIMIT_BYTES),
        name="ssm_toeplitz",
    )(ut, h0, tab)


def _layer_kernel(chunk, sub, y_by_group, x_ref, y_ref, q_ref, k_ref, kdk_ref, v_ref, s0_ref, ng_ref,
                  wg_ref, rg_ref, dm_ref, qd_ref, sd_ref, wglu_ref, bglu_ref, wo_ref, fg_ref,
                  out_ref, sfin_ref, o_ref):
    nb, tb, d = x_ref.shape
    rows = nb * tb

    @pl.when(pl.program_id(1) == 0)
    def _():
        sfin_ref[...] = s0_ref[...]

    def load_rows(ref, r0, lanes=slice(None)):
        if nb == 1:
            return ref[0, r0:r0 + sub, lanes]
        val = ref[r0 // tb:(r0 + sub) // tb, :, lanes]
        return val.reshape(sub, val.shape[-1])

    def store_rows(ref, r0, val):
        if nb == 1:
            ref[0, r0:r0 + sub, :] = val
        else:
            ref[r0 // tb:(r0 + sub) // tb] = val.reshape(sub // tb, tb, ref.shape[-1])

    rg = rg_ref[...]
    for r0 in range(0, rows, sub):
        x = load_rows(x_ref, r0)
        hn = _rms(x, ng_ref[...]).astype(BF16)
        gates = jnp.dot(hn, wg_ref[...], preferred_element_type=F32)

        for h in range(RET_HEADS):
            lanes = slice(h * HEAD_DIM, (h + 1) * HEAD_DIM)
            q, kb, kdk, vb = (load_rows(ref, r0, lanes) for ref in (q_ref, k_ref, kdk_ref, v_ref))
            dm = dm_ref[h]
            qd = qd_ref[h]
            sd = sd_ref[h]
            for c0 in range(0, sub, chunk):
                n = (r0 + c0) // tb
                r = slice(c0, c0 + chunk)
                qc, kc, vc = q[r], kb[r], vb[r]
                s = sfin_ref[n, h]
                scores = lax.dot_general(qc, kc, (((1,), (1,)), ((), ())), preferred_element_type=F32)
                p = (scores * dm).astype(BF16)
                o = jnp.dot(p, vc, preferred_element_type=F32)
                o = o + jnp.dot(qc, s.astype(BF16), preferred_element_type=F32) * qd
                sfin_ref[n, h] = sd * s + jnp.dot(kdk[r].T, vc, preferred_element_type=F32)
                o_ref[r0 + c0:r0 + c0 + chunk, lanes] = o

        g_ret = gates[:, :RET_WIDTH]
        g_ssm = gates[:, RET_WIDTH:]
        o_parts = []
        for h in range(RET_HEADS):
            lanes = slice(h * HEAD_DIM, (h + 1) * HEAD_DIM)
            o_parts.append(_rms(o_ref[r0:r0 + sub, lanes], rg[:, lanes]))
        o = jnp.concatenate(o_parts, axis=-1) * jax.nn.silu(g_ret)

        if y_by_group:
            k0 = r0 // TOEP_CHUNK
            y = jnp.concatenate(
                [y_ref[:, k0 + j].reshape(SSM_WIDTH, TOEP_CHUNK).T for j in range(sub // TOEP_CHUNK)], axis=0)
        else:
            y = load_rows(y_ref, r0)
        z = jax.nn.gelu(y)
        gate = jnp.dot(z.astype(BF16), wglu_ref[...], preferred_element_type=F32) + bglu_ref[...]
        z = z * jax.nn.sigmoid(gate)
        z = z * jax.nn.silu(g_ssm)

        mix = jnp.concatenate([o, z], axis=-1).astype(BF16)
        res = x + jnp.dot(mix, wo_ref[...], preferred_element_type=F32)
        store_rows(out_ref, r0, _rms(res, fg_ref[...]))


def _layer_call(x, y, q, k, kdk, v, s0, norm_g, w_gates, ret_norm_g, dm, qd, sd, w_glu, b_glu,
                w_out, final_g, plan):
    b, t, d = x.shape
    nb, tb, chunk, sub = plan.nb, plan.tb, plan.chunk, plan.sub
    rows = nb * tb
    c2 = lambda i, j: (0, 0)
    c3 = lambda i, j: (0, 0, 0)
    state_spec = pl.BlockSpec((nb, RET_HEADS, HEAD_DIM, HEAD_DIM), lambda i, j: (i, 0, 0, 0))
    row_spec = pl.BlockSpec((nb, tb, RET_WIDTH), lambda i, j: (i, j, 0))
    if plan.toeplitz:
        y_spec = pl.BlockSpec((SSM_GROUPS, None, tb // TOEP_CHUNK, SSM_GROUP, TOEP_CHUNK),
                              lambda i, j: (0, i, j, 0, 0))
    else:
        y_spec = pl.BlockSpec((nb, tb, SSM_WIDTH), lambda i, j: (i, j, 0))
    return pl.pallas_call(
        functools.partial(_layer_kernel, chunk, sub, plan.toeplitz),
        grid=(b // nb, t // tb),
        in_specs=[
            pl.BlockSpec((nb, tb, d), lambda i, j: (i, j, 0)),
            y_spec,
            row_spec, row_spec, row_spec, row_spec,
            state_spec,
            pl.BlockSpec((1, d), c2),
            pl.BlockSpec((d, RET_WIDTH + SSM_WIDTH), c2),
            pl.BlockSpec((1, RET_WIDTH), c2),
            pl.BlockSpec((RET_HEADS, chunk, chunk), c3),
            pl.BlockSpec((RET_HEADS, chunk, HEAD_DIM), c3),
            pl.BlockSpec((RET_HEADS, HEAD_DIM, HEAD_DIM), c3),
            pl.BlockSpec((SSM_WIDTH, SSM_WIDTH), c2),
            pl.BlockSpec((1, SSM_WIDTH), c2),
            pl.BlockSpec((D_MODEL, D_MODEL), c2),
            pl.BlockSpec((1, d), c2),
        ],
        out_specs=[
            pl.BlockSpec((nb, tb, d), lambda i, j: (i, j, 0)),
            state_spec,
        ],
        out_shape=[
            jax.ShapeDtypeStruct((b, t, d), F32),
            jax.ShapeDtypeStruct((b, RET_HEADS, HEAD_DIM, HEAD_DIM), F32),
        ],
        scratch_shapes=[pltpu.VMEM((rows, RET_WIDTH), F32)],
        compiler_params=pltpu.CompilerParams(
            dimension_semantics=("arbitrary", "arbitrary"), vmem_limit_bytes=VMEM_LIMIT_BYTES),
        name="retention_glu_layer",
    )(x, y, q, k, kdk, v, s0, norm_g, w_gates, ret_norm_g, dm, qd, sd, w_glu, b_glu, w_out, final_g)


def _rotary_tables(pos0, steps, reps):
    f = np.float32
    half = HEAD_DIM // 2
    inv = (f(ROPE_BASE) ** (-np.arange(half, dtype=f) / f(half))).astype(f)
    ang = (f(pos0) + np.arange(steps, dtype=f))[:, None] * inv[None, :]
    cos, sin = np.cos(ang), np.sin(ang)
    cos2 = np.concatenate([cos, cos], axis=-1)
    sin2 = np.concatenate([-sin, sin], axis=-1)
    return jnp.asarray(np.tile(cos2, (reps, 1)), F32), jnp.asarray(np.tile(sin2, (reps, 1)), F32)


def _retention_tables(chunk, sub):
    f = np.float32
    log_g = np.log1p(-np.power(f(2.0), f(-5.0) - np.arange(RET_HEADS, dtype=f))).astype(f)
    idx = np.arange(chunk, dtype=f)
    diff = idx[:, None] - idx[None, :]
    causal = diff >= 0
    dm = np.where(causal[None], np.exp(np.where(causal, diff, f(0.0))[None] * log_g[:, None, None]), f(0.0))
    q_decay = np.exp((idx + f(1.0))[None, :] * log_g[:, None])
    k_decay = np.exp((f(chunk) - f(1.0) - idx)[None, :] * log_g[:, None])
    s_decay = np.exp(f(chunk) * log_g)
    qd = np.broadcast_to(q_decay[:, :, None], (RET_HEADS, chunk, HEAD_DIM))
    sd = np.broadcast_to(s_decay[:, None, None], (RET_HEADS, HEAD_DIM, HEAD_DIM))
    kd = np.broadcast_to(k_decay.T[:, :, None], (chunk, RET_HEADS, HEAD_DIM)).reshape(chunk, RET_WIDTH)
    kd_rows = np.tile(kd, (sub // chunk, 1))
    return tuple(jnp.asarray(np.ascontiguousarray(a), F32) for a in (dm, qd, sd, kd_rows))


def _discretise(lambda_re, lambda_im, log_step, b_re, b_im):
    lam = lax.complex(jnp.minimum(lambda_re.astype(F32), LAMBDA_RE_MAX), lambda_im.astype(F32))
    dt = jnp.exp(log_step.astype(F32))[:, None]
    lam_bar = jnp.exp(lam * dt)
    b_bar = ((lam_bar - 1.0) / lam)[..., None] * lax.complex(b_re.astype(F32), b_im.astype(F32))
    return lam_bar, b_bar


def _scan_tables(lam_bar, b_bar, c_re, c_im):
    gl = SSM_GROUPS // SSM_HALVES
    eye = jnp.eye(gl, dtype=F32)
    ar = jnp.real(lam_bar).reshape(SSM_HALVES, HALF_RI)
    ai = jnp.imag(lam_bar).reshape(SSM_HALVES, HALF_RI)

    def place_b(m):
        m = m.reshape(SSM_HALVES, gl, SSM_STATE, SSM_GROUP).transpose(0, 1, 3, 2)
        placed = m[:, :, :, None, :] * eye[None, :, None, :, None]
        return placed.reshape(SSM_HALVES, HALF_IN, HALF_RI)

    def place_c(m):
        m = m.reshape(SSM_HALVES, gl, SSM_GROUP, SSM_STATE).transpose(0, 3, 1, 2)
        placed = m[:, None, :, :, :] * eye[None, :, None, :, None]
        return placed.reshape(SSM_HALVES, HALF_RI, HALF_IN)

    bd = jnp.concatenate([place_b(jnp.real(b_bar)), place_b(jnp.imag(b_bar))], axis=-1).astype(BF16)
    cd = jnp.concatenate([place_c(c_re.astype(F32)), place_c(-c_im.astype(F32))], axis=1).astype(BF16)
    return ar, ai, bd, cd


def _toeplitz_tables(lam_bar, b_bar, c_re, c_im, d):
    g, p = lam_bar.shape
    L = TOEP_CHUNK
    lre, lim = jnp.real(lam_bar), jnp.imag(lam_bar)
    bre, bim = jnp.real(b_bar), jnp.imag(b_bar)
    cre, cim = c_re.astype(F32), c_im.astype(F32)
    lcr = jnp.broadcast_to(lre[:, :, None], (g, p, L))
    lci = jnp.broadcast_to(lim[:, :, None], (g, p, L))
    lrr = jnp.concatenate([lre, lre], axis=-1)[:, None, :]
    lri = jnp.concatenate([lim, lim], axis=-1)[:, None, :]
    cb_re = cre[:, None, :, :] * bre.transpose(0, 2, 1)[:, :, None, :] - cim[:, None, :, :] * bim.transpose(0, 2, 1)[:, :, None, :]
    cb_im = cre[:, None, :, :] * bim.transpose(0, 2, 1)[:, :, None, :] + cim[:, None, :, :] * bre.transpose(0, 2, 1)[:, :, None, :]
    cb = jnp.concatenate([cb_re, cb_im], axis=-1).reshape(g, SSM_GROUP * SSM_GROUP, 2 * p)
    bt_re, bt_im = bre.transpose(0, 2, 1), bim.transpose(0, 2, 1)
    bb1 = jnp.concatenate([bt_re, bt_im], axis=-1)
    bb2 = jnp.concatenate([-bt_im, bt_re], axis=-1)
    ct_re, ct_im = cre.transpose(0, 2, 1), cim.transpose(0, 2, 1)
    col_a = jnp.concatenate([ct_re, -ct_re], axis=1)
    col_b = jnp.concatenate([-ct_im, -ct_im], axis=1)
    ct = jnp.concatenate([col_a, col_b], axis=-1)
    dtab = jnp.broadcast_to(d.astype(F32).reshape(g, SSM_GROUP, 1), (g, SSM_GROUP, L))

    def pad(a, rows):
        return jnp.pad(a, ((0, 0), (0, rows - a.shape[1]), (0, L - a.shape[2])))

    parts = (lcr, lci, cb, bb1, bb2, dtab, lrr, lri, ct)
    starts = [lo for lo, _ in _TOEP_TAB_ROWS] + [_TOEP_TAB_HEIGHT]
    return jnp.concatenate([pad(a, starts[i + 1] - starts[i]) for i, a in enumerate(parts)], axis=1)


def _state_to_lanes(re, im):
    b = re.shape[0]
    re = re.astype(F32).reshape(b, SSM_HALVES, HALF_RI)
    im = im.astype(F32).reshape(b, SSM_HALVES, HALF_RI)
    return jnp.concatenate([re, im], axis=-1).reshape(b, N_STATE)


def _lanes_to_state(h):
    b = h.shape[0]
    h = h.reshape(b, SSM_HALVES, 2, HALF_RI)
    return (h[:, :, 0].reshape(b, SSM_GROUPS, SSM_STATE), h[:, :, 1].reshape(b, SSM_GROUPS, SSM_STATE))


class _Plan(NamedTuple):
    nb: int
    tb: int
    chunk: int
    sub: int
    toeplitz: bool
    ssm_tb: int


def _plan(b, t):
    if t >= LAYER_ROWS:
        return _Plan(nb=1, tb=LAYER_ROWS, chunk=LAYER_CHUNK, sub=LAYER_CHUNK, toeplitz=True,
                     ssm_tb=TOEP_PROJ_STEPS)
    nb = min(b, LAYER_ROWS // t)
    return _Plan(nb=nb, tb=t, chunk=t, sub=nb * t, toeplitz=False, ssm_tb=t)


def _stream_step(x, pos0, ret_s0, h0_re, h0_im, weights):
    (norm_g, w_qkv, w_gates, w_u, ret_norm_g, lam_bar, b_bar, c_re, c_im, ssm_d, w_glu, b_glu, w_out,
     final_g) = weights
    b, t, _ = x.shape
    plan = _plan(b, t)
    cos, sin = _rotary_tables(pos0, t, plan.nb)
    dm, qd, sd, kd_rows = _retention_tables(plan.chunk, plan.sub)
    front = _front_call(x, norm_g, w_u.T if plan.toeplitz else None, w_qkv, cos, sin, kd_rows,
                        plan.nb, plan.tb, plan.sub, plan.toeplitz)
    if plan.toeplitz:
        ut, q, k, kdk, v = front
        h0 = jnp.concatenate([h0_re, h0_im], axis=-1).astype(F32).transpose(1, 0, 2)
        y, h_fin = _toeplitz_call(ut, h0, _toeplitz_tables(lam_bar, b_bar, c_re, c_im, ssm_d))
        h_fin = h_fin.transpose(1, 0, 2)
        h_re, h_im = h_fin[..., :SSM_STATE], h_fin[..., SSM_STATE:]
    else:
        q, k, kdk, v = front
        ar, ai, bd, cd = _scan_tables(lam_bar, b_bar, c_re, c_im)
        y, h_fin = _ssm_call(x, norm_g, w_u, _state_to_lanes(h0_re, h0_im), ar, ai, bd, cd,
                             ssm_d[None, :].astype(F32), plan.ssm_tb)
        h_re, h_im = _lanes_to_state(h_fin)
    out, s_fin = _layer_call(x, y, q, k, kdk, v, ret_s0, norm_g, w_gates, ret_norm_g, dm, qd, sd,
                             w_glu, b_glu, w_out, final_g, plan)
    return out, s_fin, h_re, h_im


def kernel(x_prompt, x_sample, state_ret, state_ssm_re, state_ssm_im, norm_g, w_in, ret_norm_g,
           ssm_lambda_re, ssm_lambda_im, ssm_log_step, ssm_b_re, ssm_b_im, ssm_c_re, ssm_c_im,
           ssm_d, w_glu, b_glu, w_out, final_norm_g):
    depth = w_in.shape[0]
    assert depth == 1, "single-layer trunk"
    bp, tp, _ = x_prompt.shape
    bs, ts, _ = x_sample.shape
    l = 0
    w = w_in[l]
    u_lo = 4 * RET_WIDTH
    w_qkv = w[:, :3 * RET_WIDTH].astype(BF16)
    w_gates = jnp.concatenate([w[:, 3 * RET_WIDTH:u_lo], w[:, u_lo + SSM_WIDTH:]], axis=1).astype(BF16)
    w_u = w[:, u_lo:u_lo + SSM_WIDTH].astype(BF16)
    lam_bar, b_bar = _discretise(ssm_lambda_re[l], ssm_lambda_im[l], ssm_log_step[l], ssm_b_re[l], ssm_b_im[l])
    weights = (
        norm_g[l][None, :].astype(F32), w_qkv, w_gates, w_u, ret_norm_g[l].reshape(1, RET_WIDTH).astype(F32),
        lam_bar, b_bar, ssm_c_re[l], ssm_c_im[l], ssm_d[l],
        w_glu[l].astype(BF16), b_glu[l][None, :].astype(F32),
        w_out[l].astype(BF16), final_norm_g[None, :].astype(F32),
    )

    ret0_p = jnp.zeros((bp, RET_HEADS, HEAD_DIM, HEAD_DIM), F32)
    h0_p = jnp.zeros((bp, SSM_GROUPS, SSM_STATE), F32)
    y_p, r_p, hp_re, hp_im = _stream_step(x_prompt, 0, ret0_p, h0_p, h0_p, weights)
    y_s, r_s, hs_re, hs_im = _stream_step(x_sample, PAST_LEN, state_ret[l].astype(F32),
                                          state_ssm_re[l], state_ssm_im[l], weights)
    return (y_p, y_s, r_p[None], hp_re[None], hp_im[None], r_s[None], hs_re[None], hs_im[None])
```

```python
import functools
from typing import NamedTuple

import jax
import jax.numpy as jnp
import numpy as np
from jax import lax
from jax.experimental import pallas as pl
from jax.experimental.pallas import tpu as pltpu

D_MODEL = 1024
RET_HEADS = 4
HEAD_DIM = 128
RET_WIDTH = RET_HEADS * HEAD_DIM
SSM_WIDTH = 512
SSM_GROUP = 16
SSM_GROUPS = 32
SSM_STATE = 64
ROPE_BASE = 10000.0
EPS = 1e-6
LAMBDA_RE_MAX = -1e-4
PAST_LEN = 2048

LAYER_ROWS = 1024
LAYER_CHUNK = 256
TOEP_CHUNK = 128
TOEP_LANES = SSM_GROUP * TOEP_CHUNK

SSM_HALVES = 2
HALF_IN = SSM_WIDTH // SSM_HALVES
HALF_RI = (SSM_GROUPS // SSM_HALVES) * SSM_STATE
HALF_STATE = 2 * HALF_RI
N_STATE = SSM_HALVES * HALF_STATE
SCAN_LANES = 512

VMEM_LIMIT_BYTES = 56 * 1024 * 1024

F32 = jnp.float32
BF16 = jnp.bfloat16


def _rms(x, g):
    return x * lax.rsqrt(jnp.mean(x * x, axis=-1, keepdims=True) + EPS) * g


def _ssm_kernel(x_ref, g_ref, wu_ref, h0_ref, ar_ref, ai_ref, bd_ref, cd_ref, d_ref, y_ref, hfin_ref,
                ubt_ref, utb_ref, bu_ref, hs_ref, ytb_ref):
    nb, tb, d_model = x_ref.shape
    w = SSM_WIDTH
    rows = nb * tb

    @pl.when(pl.program_id(0) == 0)
    def _():
        hfin_ref[...] = h0_ref[...]

    hn = _rms(x_ref[...].reshape(rows, d_model), g_ref[...]).astype(BF16)
    ubt_ref[...] = jnp.dot(hn, wu_ref[...], preferred_element_type=F32).reshape(nb, tb, w)
    for t in range(tb):
        utb_ref[t] = ubt_ref[:, t, :]
    u = utb_ref[...].reshape(rows, w)
    ub = u.astype(BF16)

    for h in range(SSM_HALVES):
        cols = slice(h * HALF_IN, (h + 1) * HALF_IN)
        yh = d_ref[:, cols] * u[:, cols]
        for q in range(HALF_RI // SCAN_LANES):
            re_h = q * SCAN_LANES
            im_h = HALF_RI + re_h
            re = h * HALF_STATE + re_h
            im = h * HALF_STATE + im_h
            bu_ref[:, re:re + SCAN_LANES] = jnp.dot(
                ub[:, cols], bd_ref[h, :, re_h:re_h + SCAN_LANES], preferred_element_type=F32)
            bu_ref[:, im:im + SCAN_LANES] = jnp.dot(
                ub[:, cols], bd_ref[h, :, im_h:im_h + SCAN_LANES], preferred_element_type=F32)
            ar = jnp.broadcast_to(ar_ref[h:h + 1, re_h:re_h + SCAN_LANES], (nb, SCAN_LANES))
            ai = jnp.broadcast_to(ai_ref[h:h + 1, re_h:re_h + SCAN_LANES], (nb, SCAN_LANES))
            hr = hfin_ref[:, re:re + SCAN_LANES]
            hi = hfin_ref[:, im:im + SCAN_LANES]
            for t in range(tb):
                r = slice(t * nb, (t + 1) * nb)
                hr, hi = (ar * hr - ai * hi + bu_ref[r, re:re + SCAN_LANES],
                          ar * hi + ai * hr + bu_ref[r, im:im + SCAN_LANES])
                hs_ref[r, re:re + SCAN_LANES] = hr.astype(BF16)
                hs_ref[r, im:im + SCAN_LANES] = hi.astype(BF16)
            hfin_ref[:, re:re + SCAN_LANES] = hr
            hfin_ref[:, im:im + SCAN_LANES] = hi
            yh = yh + jnp.dot(hs_ref[:, re:re + SCAN_LANES], cd_ref[h, re_h:re_h + SCAN_LANES, :],
                              preferred_element_type=F32)
            yh = yh + jnp.dot(hs_ref[:, im:im + SCAN_LANES], cd_ref[h, im_h:im_h + SCAN_LANES, :],
                              preferred_element_type=F32)
        ytb_ref[:, :, cols] = yh.reshape(tb, nb, HALF_IN)
    for b in range(nb):
        y_ref[b] = ytb_ref[:, b, :]


def _ssm_call(x, norm_g, w_u, h0, ar, ai, bd, cd, d, tb):
    nb, t, d_model = x.shape
    w = SSM_WIDTH
    rows = nb * tb
    n_blocks = t // tb
    c2 = lambda i: (0, 0)
    c3 = lambda i: (0, 0, 0)
    return pl.pallas_call(
        _ssm_kernel,
        grid=(n_blocks,),
        in_specs=[
            pl.BlockSpec((nb, tb, d_model), lambda i: (0, i, 0)),
            pl.BlockSpec((1, d_model), c2),
            pl.BlockSpec((d_model, w), c2),
            pl.BlockSpec((nb, N_STATE), c2),
            pl.BlockSpec((SSM_HALVES, HALF_RI), c2),
            pl.BlockSpec((SSM_HALVES, HALF_RI), c2),
            pl.BlockSpec((SSM_HALVES, HALF_IN, HALF_STATE), c3),
            pl.BlockSpec((SSM_HALVES, HALF_STATE, HALF_IN), c3),
            pl.BlockSpec((1, w), c2),
        ],
        out_specs=[
            pl.BlockSpec((nb, tb, w), lambda i: (0, i, 0)),
            pl.BlockSpec((nb, N_STATE), c2),
        ],
        out_shape=[
            jax.ShapeDtypeStruct((nb, t, w), F32),
            jax.ShapeDtypeStruct((nb, N_STATE), F32),
        ],
        scratch_shapes=[
            pltpu.VMEM((nb, tb, w), F32),
            pltpu.VMEM((tb, nb, w), F32),
            pltpu.VMEM((rows, N_STATE), F32),
            pltpu.VMEM((rows, N_STATE), BF16),
            pltpu.VMEM((tb, nb, w), F32),
        ],
        compiler_params=pltpu.CompilerParams(
            dimension_semantics=("arbitrary",), vmem_limit_bytes=VMEM_LIMIT_BYTES),
        name="ssm_scan",
    )(x, norm_g, w_u, h0, ar, ai, bd, cd, d)


def _front_kernel(emit_ut, sub, x_ref, g_ref, *refs):
    if emit_ut:
        wut_ref, wqkv_ref, cos_ref, sin_ref, kd_ref, ut_ref, q_ref, k_ref, kdk_ref, v_ref = refs
    else:
        wqkv_ref, cos_ref, sin_ref, kd_ref, q_ref, k_ref, kdk_ref, v_ref = refs
    nb, tb, d = x_ref.shape
    rows = nb * tb

    def load_rows(ref, r0):
        if nb == 1:
            return ref[0, r0:r0 + sub, :]
        return ref[r0 // tb:(r0 + sub) // tb].reshape(sub, ref.shape[-1])

    def store_rows(ref, r0, lanes, val):
        if nb == 1:
            ref[0, r0:r0 + sub, lanes] = val
        else:
            ref[r0 // tb:(r0 + sub) // tb, :, lanes] = val.reshape(sub // tb, tb, val.shape[-1])

    hns = []
    for r0 in range(0, rows, sub):
        hn = _rms(load_rows(x_ref, r0), g_ref[...]).astype(BF16)
        hns.append(hn)
        qkv = jnp.dot(hn, wqkv_ref[...], preferred_element_type=F32)
        cos = cos_ref[r0:r0 + sub, :]
        sin = sin_ref[r0:r0 + sub, :]

        def rotary(a):
            return a * cos + pltpu.roll(a, HEAD_DIM // 2, 1) * sin

        for h in range(RET_HEADS):
            lanes = slice(h * HEAD_DIM, (h + 1) * HEAD_DIM)
            k = rotary(qkv[:, RET_WIDTH + h * HEAD_DIM:RET_WIDTH + (h + 1) * HEAD_DIM]) * (HEAD_DIM ** -0.5)
            store_rows(q_ref, r0, lanes, rotary(qkv[:, lanes]).astype(BF16))
            store_rows(k_ref, r0, lanes, k.astype(BF16))
            store_rows(kdk_ref, r0, lanes, (k * kd_ref[:, lanes]).astype(BF16))
            store_rows(v_ref, r0, lanes,
                       qkv[:, 2 * RET_WIDTH + h * HEAD_DIM:2 * RET_WIDTH + (h + 1) * HEAD_DIM].astype(BF16))

    if emit_ut:
        nk = rows // TOEP_CHUNK
        hn = jnp.concatenate(hns, axis=0)
        ut = lax.dot_general(wut_ref[...], hn, (((1,), (1,)), ((), ())), preferred_element_type=F32)
        for g in range(SSM_GROUPS):
            grp = slice(g * SSM_GROUP, (g + 1) * SSM_GROUP)
            slabs = jnp.stack([ut[grp, kk * TOEP_CHUNK:(kk + 1) * TOEP_CHUNK] for kk in range(nk)])
            by_channel = jnp.swapaxes(slabs, 0, 1)
            for c in range(SSM_GROUP):
                ut_ref[g, :, c * TOEP_CHUNK:(c + 1) * TOEP_CHUNK] = by_channel[c]


def _front_call(x, norm_g, w_ut, w_qkv, cos, sin, kd_rows, nb, tb, sub, emit_ut):
    b, t, d = x.shape
    rows = nb * tb
    c2 = lambda i, j: (0, 0)
    row_spec = pl.BlockSpec((nb, tb, RET_WIDTH), lambda i, j: (i, j, 0))
    in_specs = [pl.BlockSpec((nb, tb, d), lambda i, j: (i, j, 0)), pl.BlockSpec((1, d), c2)]
    operands = [x, norm_g]
    out_specs, out_shape = [], []
    if emit_ut:
        in_specs.append(pl.BlockSpec((SSM_WIDTH, d), c2))
        operands.append(w_ut)
        out_specs.append(pl.BlockSpec((SSM_GROUPS, None, tb // TOEP_CHUNK, TOEP_LANES), lambda i, j: (0, i, j, 0)))
        out_shape.append(jax.ShapeDtypeStruct((SSM_GROUPS, b, t // TOEP_CHUNK, TOEP_LANES), F32))
    in_specs += [
        pl.BlockSpec((d, 3 * RET_WIDTH), c2),
        pl.BlockSpec((rows, HEAD_DIM), lambda i, j: (j, 0)),
        pl.BlockSpec((rows, HEAD_DIM), lambda i, j: (j, 0)),
        pl.BlockSpec((sub, RET_WIDTH), c2),
    ]
    operands += [w_qkv, cos, sin, kd_rows]
    out_specs += [row_spec] * 4
    out_shape += [jax.ShapeDtypeStruct((b, t, RET_WIDTH), BF16)] * 4
    return pl.pallas_call(
        functools.partial(_front_kernel, emit_ut, sub),
        grid=(b // nb, t // tb),
        in_specs=in_specs,
        out_specs=out_specs,
        out_shape=out_shape,
        compiler_params=pltpu.CompilerParams(
            dimension_semantics=("arbitrary", "arbitrary"), vmem_limit_bytes=VMEM_LIMIT_BYTES),
        name="qkv_rotary_front",
    )(*operands)


def _powers(base_re, base_im, exponent, n_bits):
    pr = jnp.ones_like(base_re)
    pi = jnp.zeros_like(base_re)
    br, bi = base_re, base_im
    for bit in range(n_bits):
        take = ((exponent >> bit) & 1) == 1
        pr, pi = jnp.where(take, pr * br - pi * bi, pr), jnp.where(take, pr * bi + pi * br, pi)
        br, bi = br * br - bi * bi, 2.0 * br * bi
    return pr, pi


def _split_bf16(x):
    hi = x.astype(BF16)
    r = x - hi.astype(F32)
    mid = r.astype(BF16)
    lo = (r - mid.astype(F32)).astype(BF16)
    return hi, mid, lo


def _dot_f32(a, b):
    a3, b3 = _split_bf16(a), _split_bf16(b)
    acc = None
    for i in range(3):
        for j in range(3 - i):
            term = jnp.dot(a3[i], b3[j], preferred_element_type=F32)
            acc = term if acc is None else acc + term
    return acc


def _toeplitz_kernel(u_ref, h0_ref, lcr_ref, lci_ref, lrr_ref, lri_ref, cb_ref, bb1_ref, bb2_ref,
                     ct_ref, d_ref, y_ref, hfin_ref, a_ref, wb_ref, wc_ref, t_ref, ys_ref, yi_ref):
    nb, nk, _ = u_ref.shape
    rows = nb * nk
    L = TOEP_CHUNK
    n_bits = L.bit_length() - 1
    sub_i = lax.broadcasted_iota(jnp.int32, (L, L), 0)
    lane_i = lax.broadcasted_iota(jnp.int32, (L, L), 1)

    lcr, lci = lcr_ref[...], lci_ref[...]
    lane_p = lax.broadcasted_iota(jnp.int32, lcr.shape, 1)
    p0r, p0i = _powers(lcr, lci, lane_p, n_bits)
    p1r, p1i = p0r * lcr - p0i * lci, p0r * lci + p0i * lcr

    taps = _dot_f32(cb_ref[...], jnp.concatenate([p0r, -p0i], axis=0))

    lrr = jnp.broadcast_to(lrr_ref[...], (L, L))
    lri = jnp.broadcast_to(lri_ref[...], (L, L))
    qr, qi = _powers(lrr, lri, (L - 1) - sub_i, n_bits)
    for c in range(SSM_GROUP):
        wb_ref[c * L:(c + 1) * L, :] = (qr * bb1_ref[c:c + 1, :] + qi * bb2_ref[c:c + 1, :]).astype(BF16)
    pa = jnp.concatenate([p1r, p1i], axis=0)
    pb = jnp.concatenate([p1i, p1r], axis=0)
    for c in range(SSM_GROUP):
        wc_ref[:, c * L:(c + 1) * L] = (
            jnp.broadcast_to(ct_ref[:, c:c + 1], (L, L)) * pa
            + jnp.broadcast_to(ct_ref[:, SSM_GROUP + c:SSM_GROUP + c + 1], (L, L)) * pb).astype(BF16)

    a_ref[...] = u_ref[...].reshape(rows, TOEP_LANES).astype(BF16)

    s_in = jnp.dot(a_ref[...], wb_ref[...], preferred_element_type=F32)
    s_in = jnp.swapaxes(s_in.reshape(nb, nk, L), 0, 1)
    l128r = qr[0:1] * lrr[0:1] - qi[0:1] * lri[0:1]
    l128i = qr[0:1] * lri[0:1] + qi[0:1] * lrr[0:1]
    half = L // 2
    is_re = lane_i[0:1] < half
    mul_same = jnp.broadcast_to(l128r, (nb, L))
    mul_swap = jnp.broadcast_to(jnp.where(is_re, -l128i, l128i), (nb, L))
    s_sw = pltpu.roll(s_in.reshape(nk * nb, L), half, 1).reshape(nk, nb, L)
    h = h0_ref[...]
    h_sw = pltpu.roll(h, half, 1)
    h_prev = []
    for k in range(nk):
        h_prev.append(h)
        h, h_sw = (h * mul_same + h_sw * mul_swap + s_in[k],
                   h_sw * mul_same - h * mul_swap + s_sw[k])
    hfin_ref[...] = h
    h_prev = jnp.swapaxes(jnp.stack(h_prev), 0, 1).reshape(rows, L).astype(BF16)

    causal = lane_i >= sub_i
    for pair in range(SSM_GROUP // 2):
        slot = pair
        for ci in range(SSM_GROUP):
            for cc in range(2):
                r = ci * SSM_GROUP + 2 * pair + cc
                tap = jnp.broadcast_to(taps[r:r + 1, :], (L, L))
                toep = jnp.where(causal, pltpu.roll(tap, 0, 1, stride=1, stride_axis=0), 0.0)
                t_ref[slot, ci * L:(ci + 1) * L, cc * L:(cc + 1) * L] = toep.astype(BF16)
        yv = jnp.dot(a_ref[...], t_ref[slot], preferred_element_type=F32)
        for cc in range(2):
            c = 2 * pair + cc
            uc = u_ref[:, :, c * L:(c + 1) * L]
            ys_ref[c] = yv[:, cc * L:(cc + 1) * L].reshape(nb, nk, L) + d_ref[c:c + 1, :] * uc
    inter = jnp.dot(h_prev, wc_ref[...], preferred_element_type=F32)
    for c in range(SSM_GROUP):
        yi_ref[c] = inter[:, c * L:(c + 1) * L].reshape(nb, nk, L)
    for b in range(nb):
        y_ref[b] = jnp.swapaxes(ys_ref[:, b] + yi_ref[:, b], 0, 1)


def _toeplitz_call(ut, h0, tabs):
    g, nb, nk, lanes = ut.shape
    L = TOEP_CHUNK
    rows = nb * nk
    lcr, lci, lrr, lri, cb, bb1, bb2, ct, dtab = tabs

    def per_group(*shape):
        return pl.BlockSpec((None,) + shape, lambda i: (i,) + (0,) * len(shape))

    return pl.pallas_call(
        _toeplitz_kernel,
        grid=(g,),
        in_specs=[
            per_group(nb, nk, lanes), per_group(nb, L),
            per_group(SSM_STATE, L), per_group(SSM_STATE, L), per_group(1, L), per_group(1, L),
            per_group(SSM_GROUP * SSM_GROUP, L), per_group(SSM_GROUP, L), per_group(SSM_GROUP, L),
            per_group(L, 2 * SSM_GROUP), per_group(SSM_GROUP, L),
        ],
        out_specs=[per_group(nb, nk, SSM_GROUP, L), per_group(nb, L)],
        out_shape=[
            jax.ShapeDtypeStruct((g, nb, nk, SSM_GROUP, L), F32),
            jax.ShapeDtypeStruct((g, nb, L), F32),
        ],
        scratch_shapes=[
            pltpu.VMEM((rows, lanes), BF16),
            pltpu.VMEM((lanes, L), BF16),
            pltpu.VMEM((L, lanes), BF16),
            pltpu.VMEM((SSM_GROUP // 2, lanes, 2 * L), BF16),
            pltpu.VMEM((SSM_GROUP, nb, nk, L), F32),
            pltpu.VMEM((SSM_GROUP, nb, nk, L), F32),
        ],
        compiler_params=pltpu.CompilerParams(
            dimension_semantics=("arbitrary",), vmem_limit_bytes=VMEM_LIMIT_BYTES),
        name="ssm_toeplitz",
    )(ut, h0, lcr, lci, lrr, lri, cb, bb1, bb2, ct, dtab)


def _layer_kernel(chunk, sub, y_by_group, x_ref, y_ref, q_ref, k_ref, kdk_ref, v_ref, s0_ref, ng_ref,
                  wg_ref, rg_ref, dm_ref, qd_ref, sd_ref, wglu_ref, bglu_ref, wo_ref, fg_ref,
                  out_ref, sfin_ref, o_ref):
    nb, tb, d = x_ref.shape
    rows = nb * tb

    @pl.when(pl.program_id(1) == 0)
    def _():
        sfin_ref[...] = s0_ref[...]

    def load_rows(ref, r0, lanes=slice(None)):
        if nb == 1:
            return ref[0, r0:r0 + sub, lanes]
        val = ref[r0 // tb:(r0 + sub) // tb, :, lanes]
        return val.reshape(sub, val.shape[-1])

    def store_rows(ref, r0, val):
        if nb == 1:
            ref[0, r0:r0 + sub, :] = val
        else:
            ref[r0 // tb:(r0 + sub) // tb] = val.reshape(sub // tb, tb, ref.shape[-1])

    rg = rg_ref[...]
    for r0 in range(0, rows, sub):
        x = load_rows(x_ref, r0)
        hn = _rms(x, ng_ref[...]).astype(BF16)
        gates = jnp.dot(hn, wg_ref[...], preferred_element_type=F32)

        for h in range(RET_HEADS):
            lanes = slice(h * HEAD_DIM, (h + 1) * HEAD_DIM)
            q, kb, kdk, vb = (load_rows(ref, r0, lanes) for ref in (q_ref, k_ref, kdk_ref, v_ref))
            dm = dm_ref[h]
            qd = qd_ref[h]
            sd = sd_ref[h]
            for c0 in range(0, sub, chunk):
                n = (r0 + c0) // tb
                r = slice(c0, c0 + chunk)
                qc, kc, vc = q[r], kb[r], vb[r]
                s = sfin_ref[n, h]
                scores = lax.dot_general(qc, kc, (((1,), (1,)), ((), ())), preferred_element_type=F32)
                p = (scores * dm).astype(BF16)
                o = jnp.dot(p, vc, preferred_element_type=F32)
                o = o + jnp.dot(qc, s.astype(BF16), preferred_element_type=F32) * qd
                sfin_ref[n, h] = sd * s + jnp.dot(kdk[r].T, vc, preferred_element_type=F32)
                o_ref[r0 + c0:r0 + c0 + chunk, lanes] = o

        g_ret = gates[:, :RET_WIDTH]
        g_ssm = gates[:, RET_WIDTH:]
        o_parts = []
        for h in range(RET_HEADS):
            lanes = slice(h * HEAD_DIM, (h + 1) * HEAD_DIM)
            o_parts.append(_rms(o_ref[r0:r0 + sub, lanes], rg[:, lanes]))
        o = jnp.concatenate(o_parts, axis=-1) * jax.nn.silu(g_ret)

        if y_by_group:
            k0 = r0 // TOEP_CHUNK
            y = jnp.concatenate(
                [y_ref[:, k0 + j].reshape(SSM_WIDTH, TOEP_CHUNK).T for j in range(sub // TOEP_CHUNK)], axis=0)
        else:
            y = load_rows(y_ref, r0)
        z = jax.nn.gelu(y)
        gate = jnp.dot(z.astype(BF16), wglu_ref[...], preferred_element_type=F32) + bglu_ref[...]
        z = z * jax.nn.sigmoid(gate)
        z = z * jax.nn.silu(g_ssm)

        mix = jnp.concatenate([o, z], axis=-1).astype(BF16)
        res = x + jnp.dot(mix, wo_ref[...], preferred_element_type=F32)
        store_rows(out_ref, r0, _rms(res, fg_ref[...]))


def _layer_call(x, y, q, k, kdk, v, s0, norm_g, w_gates, ret_norm_g, dm, qd, sd, w_glu, b_glu,
                w_out, final_g, plan):
    b, t, d = x.shape
    nb, tb, chunk, sub = plan.nb, plan.tb, plan.chunk, plan.sub
    rows = nb * tb
    c2 = lambda i, j: (0, 0)
    c3 = lambda i, j: (0, 0, 0)
    state_spec = pl.BlockSpec((nb, RET_HEADS, HEAD_DIM, HEAD_DIM), lambda i, j: (i, 0, 0, 0))
    row_spec = pl.BlockSpec((nb, tb, RET_WIDTH), lambda i, j: (i, j, 0))
    if plan.toeplitz:
        y_spec = pl.BlockSpec((SSM_GROUPS, None, tb // TOEP_CHUNK, SSM_GROUP, TOEP_CHUNK),
                              lambda i, j: (0, i, j, 0, 0))
    else:
        y_spec = pl.BlockSpec((nb, tb, SSM_WIDTH), lambda i, j: (i, j, 0))
    return pl.pallas_call(
        functools.partial(_layer_kernel, chunk, sub, plan.toeplitz),
        grid=(b // nb, t // tb),
        in_specs=[
            pl.BlockSpec((nb, tb, d), lambda i, j: (i, j, 0)),
            y_spec,
            row_spec, row_spec, row_spec, row_spec,
            state_spec,
            pl.BlockSpec((1, d), c2),
            pl.BlockSpec((d, RET_WIDTH + SSM_WIDTH), c2),
            pl.BlockSpec((1, RET_WIDTH), c2),
            pl.BlockSpec((RET_HEADS, chunk, chunk), c3),
            pl.BlockSpec((RET_HEADS, chunk, HEAD_DIM), c3),
            pl.BlockSpec((RET_HEADS, HEAD_DIM, HEAD_DIM), c3),
            pl.BlockSpec((SSM_WIDTH, SSM_WIDTH), c2),
            pl.BlockSpec((1, SSM_WIDTH), c2),
            pl.BlockSpec((D_MODEL, D_MODEL), c2),
            pl.BlockSpec((1, d), c2),
        ],
        out_specs=[
            pl.BlockSpec((nb, tb, d), lambda i, j: (i, j, 0)),
            state_spec,
        ],
        out_shape=[
            jax.ShapeDtypeStruct((b, t, d), F32),
            jax.ShapeDtypeStruct((b, RET_HEADS, HEAD_DIM, HEAD_DIM), F32),
        ],
        scratch_shapes=[pltpu.VMEM((rows, RET_WIDTH), F32)],
        compiler_params=pltpu.CompilerParams(
            dimension_semantics=("arbitrary", "arbitrary"), vmem_limit_bytes=VMEM_LIMIT_BYTES),
        name="retention_glu_layer",
    )(x, y, q, k, kdk, v, s0, norm_g, w_gates, ret_norm_g, dm, qd, sd, w_glu, b_glu, w_out, final_g)


def _rotary_tables(pos0, steps, reps):
    f = np.float32
    half = HEAD_DIM // 2
    inv = (f(ROPE_BASE) ** (-np.arange(half, dtype=f) / f(half))).astype(f)
    ang = (f(pos0) + np.arange(steps, dtype=f))[:, None] * inv[None, :]
    cos, sin = np.cos(ang), np.sin(ang)
    cos2 = np.concatenate([cos, cos], axis=-1)
    sin2 = np.concatenate([-sin, sin], axis=-1)
    return jnp.asarray(np.tile(cos2, (reps, 1)), F32), jnp.asarray(np.tile(sin2, (reps, 1)), F32)


def _retention_tables(chunk, sub):
    f = np.float32
    log_g = np.log1p(-np.power(f(2.0), f(-5.0) - np.arange(RET_HEADS, dtype=f))).astype(f)
    idx = np.arange(chunk, dtype=f)
    diff = idx[:, None] - idx[None, :]
    causal = diff >= 0
    dm = np.where(causal[None], np.exp(np.where(causal, diff, f(0.0))[None] * log_g[:, None, None]), f(0.0))
    q_decay = np.exp((idx + f(1.0))[None, :] * log_g[:, None])
    k_decay = np.exp((f(chunk) - f(1.0) - idx)[None, :] * log_g[:, None])
    s_decay = np.exp(f(chunk) * log_g)
    qd = np.broadcast_to(q_decay[:, :, None], (RET_HEADS, chunk, HEAD_DIM))
    sd = np.broadcast_to(s_decay[:, None, None], (RET_HEADS, HEAD_DIM, HEAD_DIM))
    kd = np.broadcast_to(k_decay.T[:, :, None], (chunk, RET_HEADS, HEAD_DIM)).reshape(chunk, RET_WIDTH)
    kd_rows = np.tile(kd, (sub // chunk, 1))
    return tuple(jnp.asarray(np.ascontiguousarray(a), F32) for a in (dm, qd, sd, kd_rows))


def _discretise(lambda_re, lambda_im, log_step, b_re, b_im):
    lam = lax.complex(jnp.minimum(lambda_re.astype(F32), LAMBDA_RE_MAX), lambda_im.astype(F32))
    dt = jnp.exp(log_step.astype(F32))[:, None]
    lam_bar = jnp.exp(lam * dt)
    b_bar = ((lam_bar - 1.0) / lam)[..., None] * lax.complex(b_re.astype(F32), b_im.astype(F32))
    return lam_bar, b_bar


def _scan_tables(lam_bar, b_bar, c_re, c_im):
    gl = SSM_GROUPS // SSM_HALVES
    eye = jnp.eye(gl, dtype=F32)
    ar = jnp.real(lam_bar).reshape(SSM_HALVES, HALF_RI)
    ai = jnp.imag(lam_bar).reshape(SSM_HALVES, HALF_RI)

    def place_b(m):
        m = m.reshape(SSM_HALVES, gl, SSM_STATE, SSM_GROUP).transpose(0, 1, 3, 2)
        placed = m[:, :, :, None, :] * eye[None, :, None, :, None]
        return placed.reshape(SSM_HALVES, HALF_IN, HALF_RI)

    def place_c(m):
        m = m.reshape(SSM_HALVES, gl, SSM_GROUP, SSM_STATE).transpose(0, 3, 1, 2)
        placed = m[:, None, :, :, :] * eye[None, :, None, :, None]
        return placed.reshape(SSM_HALVES, HALF_RI, HALF_IN)

    bd = jnp.concatenate([place_b(jnp.real(b_bar)), place_b(jnp.imag(b_bar))], axis=-1).astype(BF16)
    cd = jnp.concatenate([place_c(c_re.astype(F32)), place_c(-c_im.astype(F32))], axis=1).astype(BF16)
    return ar, ai, bd, cd


def _toeplitz_tables(lam_bar, b_bar, c_re, c_im, d):
    g, p = lam_bar.shape
    L = TOEP_CHUNK
    lre, lim = jnp.real(lam_bar), jnp.imag(lam_bar)
    bre, bim = jnp.real(b_bar), jnp.imag(b_bar)
    cre, cim = c_re.astype(F32), c_im.astype(F32)
    lcr = jnp.broadcast_to(lre[:, :, None], (g, p, L))
    lci = jnp.broadcast_to(lim[:, :, None], (g, p, L))
    lrr = jnp.concatenate([lre, lre], axis=-1)[:, None, :]
    lri = jnp.concatenate([lim, lim], axis=-1)[:, None, :]
    cb_re = cre[:, None, :, :] * bre.transpose(0, 2, 1)[:, :, None, :] - cim[:, None, :, :] * bim.transpose(0, 2, 1)[:, :, None, :]
    cb_im = cre[:, None, :, :] * bim.transpose(0, 2, 1)[:, :, None, :] + cim[:, None, :, :] * bre.transpose(0, 2, 1)[:, :, None, :]
    cb = jnp.concatenate([cb_re, cb_im], axis=-1).reshape(g, SSM_GROUP * SSM_GROUP, 2 * p)
    bt_re, bt_im = bre.transpose(0, 2, 1), bim.transpose(0, 2, 1)
    bb1 = jnp.concatenate([bt_re, bt_im], axis=-1)
    bb2 = jnp.concatenate([-bt_im, bt_re], axis=-1)
    ct_re, ct_im = cre.transpose(0, 2, 1), cim.transpose(0, 2, 1)
    col_a = jnp.concatenate([ct_re, -ct_re], axis=1)
    col_b = jnp.concatenate([-ct_im, -ct_im], axis=1)
    ct = jnp.concatenate([col_a, col_b], axis=-1)
    dtab = jnp.broadcast_to(d.astype(F32).reshape(g, SSM_GROUP, 1), (g, SSM_GROUP, L))
    return lcr, lci, lrr, lri, cb, bb1, bb2, ct, dtab


def _state_to_lanes(re, im):
    b = re.shape[0]
    re = re.astype(F32).reshape(b, SSM_HALVES, HALF_RI)
    im = im.astype(F32).reshape(b, SSM_HALVES, HALF_RI)
    return jnp.concatenate([re, im], axis=-1).reshape(b, N_STATE)


def _lanes_to_state(h):
    b = h.shape[0]
    h = h.reshape(b, SSM_HALVES, 2, HALF_RI)
    return (h[:, :, 0].reshape(b, SSM_GROUPS, SSM_STATE), h[:, :, 1].reshape(b, SSM_GROUPS, SSM_STATE))


class _Plan(NamedTuple):
    nb: int
    tb: int
    chunk: int
    sub: int
    toeplitz: bool


def _plan(b, t):
    if t >= LAYER_ROWS:
        assert t % LAYER_ROWS == 0 and LAYER_ROWS % LAYER_CHUNK == 0 and LAYER_CHUNK % TOEP_CHUNK == 0
        return _Plan(nb=1, tb=LAYER_ROWS, chunk=LAYER_CHUNK, sub=LAYER_CHUNK, toeplitz=True)
    nb = min(b, LAYER_ROWS // t)
    assert b % nb == 0
    return _Plan(nb=nb, tb=t, chunk=t, sub=nb * t, toeplitz=False)


def _stream_step(x, pos0, ret_s0, h0_re, h0_im, weights):
    (norm_g, w_qkv, w_gates, w_u, ret_norm_g, lam_bar, b_bar, c_re, c_im, ssm_d, w_glu, b_glu, w_out,
     final_g) = weights
    b, t, _ = x.shape
    plan = _plan(b, t)
    cos, sin = _rotary_tables(pos0, t, plan.nb)
    dm, qd, sd, kd_rows = _retention_tables(plan.chunk, plan.sub)
    front = _front_call(x, norm_g, w_u.T if plan.toeplitz else None, w_qkv, cos, sin, kd_rows,
                        plan.nb, plan.tb, plan.sub, plan.toeplitz)
    if plan.toeplitz:
        ut, q, k, kdk, v = front
        h0 = jnp.concatenate([h0_re, h0_im], axis=-1).astype(F32).transpose(1, 0, 2)
        y, h_fin = _toeplitz_call(ut, h0, _toeplitz_tables(lam_bar, b_bar, c_re, c_im, ssm_d))
        h_fin = h_fin.transpose(1, 0, 2)
        h_re, h_im = h_fin[..., :SSM_STATE], h_fin[..., SSM_STATE:]
    else:
        q, k, kdk, v = front
        ar, ai, bd, cd = _scan_tables(lam_bar, b_bar, c_re, c_im)
        y, h_fin = _ssm_call(x, norm_g, w_u, _state_to_lanes(h0_re, h0_im), ar, ai, bd, cd,
                             ssm_d[None, :].astype(F32), plan.tb)
        h_re, h_im = _lanes_to_state(h_fin)
    out, s_fin = _layer_call(x, y, q, k, kdk, v, ret_s0, norm_g, w_gates, ret_norm_g, dm, qd, sd,
                             w_glu, b_glu, w_out, final_g, plan)
    return out, s_fin, h_re, h_im


def kernel(x_prompt, x_sample, state_ret, state_ssm_re, state_ssm_im, norm_g, w_in, ret_norm_g,
           ssm_lambda_re, ssm_lambda_im, ssm_log_step, ssm_b_re, ssm_b_im, ssm_c_re, ssm_c_im,
           ssm_d, w_glu, b_glu, w_out, final_norm_g):
    depth = w_in.shape[0]
    assert depth == 1, "single-layer trunk"
    bp, tp, _ = x_prompt.shape
    bs, ts, _ = x_sample.shape
    l = 0
    w = w_in[l]
    u_lo = 4 * RET_WIDTH
    w_qkv = w[:, :3 * RET_WIDTH].astype(BF16)
    w_gates = jnp.concatenate([w[:, 3 * RET_WIDTH:u_lo], w[:, u_lo + SSM_WIDTH:]], axis=1).astype(BF16)
    w_u = w[:, u_lo:u_lo + SSM_WIDTH].astype(BF16)
    lam_bar, b_bar = _discretise(ssm_lambda_re[l], ssm_lambda_im[l], ssm_log_step[l], ssm_b_re[l], ssm_b_im[l])
    weights = (
        norm_g[l][None, :].astype(F32), w_qkv, w_gates, w_u, ret_norm_g[l].reshape(1, RET_WIDTH).astype(F32),
        lam_bar, b_bar, ssm_c_re[l], ssm_c_im[l], ssm_d[l],
        w_glu[l].astype(BF16), b_glu[l][None, :].astype(F32),
        w_out[l].astype(BF16), final_norm_g[None, :].astype(F32),
    )

    ret0_p = jnp.zeros((bp, RET_HEADS, HEAD_DIM, HEAD_DIM), F32)
    h0_p = jnp.zeros((bp, SSM_GROUPS, SSM_STATE), F32)
    y_p, r_p, hp_re, hp_im = _stream_step(x_prompt, 0, ret0_p, h0_p, h0_p, weights)
    y_s, r_s, hs_re, hs_im = _stream_step(x_sample, PAST_LEN, state_ret[l].astype(F32),
                                          state_ssm_re[l], state_ssm_im[l], weights)
    return (y_p, y_s, r_p[None], hp_re[None], hp_im[None], r_s[None], hs_re[None], hs_im[None])
```

```python
import functools
from typing import NamedTuple

import jax
import jax.numpy as jnp
import numpy as np
from jax import lax
from jax.experimental import pallas as pl
from jax.experimental.pallas import tpu as pltpu

D_MODEL = 1024
RET_HEADS = 4
HEAD_DIM = 128
RET_WIDTH = RET_HEADS * HEAD_DIM
SSM_WIDTH = 512
SSM_GROUP = 16
SSM_GROUPS = 32
SSM_STATE = 64
ROPE_BASE = 10000.0
EPS = 1e-6
LAMBDA_RE_MAX = -1e-4
PAST_LEN = 2048

LAYER_ROWS = 1024
LAYER_CHUNK = 256
TOEP_CHUNK = 128
TOEP_LANES = SSM_GROUP * TOEP_CHUNK

SSM_HALVES = 2
HALF_IN = SSM_WIDTH // SSM_HALVES
HALF_RI = (SSM_GROUPS // SSM_HALVES) * SSM_STATE
HALF_STATE = 2 * HALF_RI
N_STATE = SSM_HALVES * HALF_STATE
SCAN_LANES = 512

VMEM_LIMIT_BYTES = 56 * 1024 * 1024

F32 = jnp.float32
BF16 = jnp.bfloat16


def _rms(x, g):
    return x * lax.rsqrt(jnp.mean(x * x, axis=-1, keepdims=True) + EPS) * g


def _ssm_kernel(x_ref, g_ref, wu_ref, h0_ref, ar_ref, ai_ref, bd_ref, cd_ref, d_ref, y_ref, hfin_ref,
                ubt_ref, utb_ref, bu_ref, hs_ref, ytb_ref):
    nb, tb, d_model = x_ref.shape
    w = SSM_WIDTH
    rows = nb * tb

    @pl.when(pl.program_id(0) == 0)
    def _():
        hfin_ref[...] = h0_ref[...]

    hn = _rms(x_ref[...].reshape(rows, d_model), g_ref[...]).astype(BF16)
    ubt_ref[...] = jnp.dot(hn, wu_ref[...], preferred_element_type=F32).reshape(nb, tb, w)
    for t in range(tb):
        utb_ref[t] = ubt_ref[:, t, :]
    u = utb_ref[...].reshape(rows, w)
    ub = u.astype(BF16)

    for h in range(SSM_HALVES):
        cols = slice(h * HALF_IN, (h + 1) * HALF_IN)
        yh = d_ref[:, cols] * u[:, cols]
        for q in range(HALF_RI // SCAN_LANES):
            re_h = q * SCAN_LANES
            im_h = HALF_RI + re_h
            re = h * HALF_STATE + re_h
            im = h * HALF_STATE + im_h
            bu_ref[:, re:re + SCAN_LANES] = jnp.dot(
                ub[:, cols], bd_ref[h, :, re_h:re_h + SCAN_LANES], preferred_element_type=F32)
            bu_ref[:, im:im + SCAN_LANES] = jnp.dot(
                ub[:, cols], bd_ref[h, :, im_h:im_h + SCAN_LANES], preferred_element_type=F32)
            ar = jnp.broadcast_to(ar_ref[h:h + 1, re_h:re_h + SCAN_LANES], (nb, SCAN_LANES))
            ai = jnp.broadcast_to(ai_ref[h:h + 1, re_h:re_h + SCAN_LANES], (nb, SCAN_LANES))
            hr = hfin_ref[:, re:re + SCAN_LANES]
            hi = hfin_ref[:, im:im + SCAN_LANES]
            for t in range(tb):
                r = slice(t * nb, (t + 1) * nb)
                hr, hi = (ar * hr - ai * hi + bu_ref[r, re:re + SCAN_LANES],
                          ar * hi + ai * hr + bu_ref[r, im:im + SCAN_LANES])
                hs_ref[r, re:re + SCAN_LANES] = hr.astype(BF16)
                hs_ref[r, im:im + SCAN_LANES] = hi.astype(BF16)
            hfin_ref[:, re:re + SCAN_LANES] = hr
            hfin_ref[:, im:im + SCAN_LANES] = hi
            yh = yh + jnp.dot(hs_ref[:, re:re + SCAN_LANES], cd_ref[h, re_h:re_h + SCAN_LANES, :],
                              preferred_element_type=F32)
            yh = yh + jnp.dot(hs_ref[:, im:im + SCAN_LANES], cd_ref[h, im_h:im_h + SCAN_LANES, :],
                              preferred_element_type=F32)
        ytb_ref[:, :, cols] = yh.reshape(tb, nb, HALF_IN)
    for b in range(nb):
        y_ref[b] = ytb_ref[:, b, :]


def _ssm_call(x, norm_g, w_u, h0, ar, ai, bd, cd, d, tb):
    nb, t, d_model = x.shape
    w = SSM_WIDTH
    rows = nb * tb
    n_blocks = t // tb
    c2 = lambda i: (0, 0)
    c3 = lambda i: (0, 0, 0)
    return pl.pallas_call(
        _ssm_kernel,
        grid=(n_blocks,),
        in_specs=[
            pl.BlockSpec((nb, tb, d_model), lambda i: (0, i, 0)),
            pl.BlockSpec((1, d_model), c2),
            pl.BlockSpec((d_model, w), c2),
            pl.BlockSpec((nb, N_STATE), c2),
            pl.BlockSpec((SSM_HALVES, HALF_RI), c2),
            pl.BlockSpec((SSM_HALVES, HALF_RI), c2),
            pl.BlockSpec((SSM_HALVES, HALF_IN, HALF_STATE), c3),
            pl.BlockSpec((SSM_HALVES, HALF_STATE, HALF_IN), c3),
            pl.BlockSpec((1, w), c2),
        ],
        out_specs=[
            pl.BlockSpec((nb, tb, w), lambda i: (0, i, 0)),
            pl.BlockSpec((nb, N_STATE), c2),
        ],
        out_shape=[
            jax.ShapeDtypeStruct((nb, t, w), F32),
            jax.ShapeDtypeStruct((nb, N_STATE), F32),
        ],
        scratch_shapes=[
            pltpu.VMEM((nb, tb, w), F32),
            pltpu.VMEM((tb, nb, w), F32),
            pltpu.VMEM((rows, N_STATE), F32),
            pltpu.VMEM((rows, N_STATE), BF16),
            pltpu.VMEM((tb, nb, w), F32),
        ],
        compiler_params=pltpu.CompilerParams(
            dimension_semantics=("arbitrary",), vmem_limit_bytes=VMEM_LIMIT_BYTES),
        name="ssm_scan",
    )(x, norm_g, w_u, h0, ar, ai, bd, cd, d)


def _front_kernel(emit_ut, sub, x_ref, g_ref, *refs):
    if emit_ut:
        wut_ref, wqkv_ref, cos_ref, sin_ref, kd_ref, ut_ref, q_ref, k_ref, kdk_ref, v_ref = refs
    else:
        wqkv_ref, cos_ref, sin_ref, kd_ref, q_ref, k_ref, kdk_ref, v_ref = refs
    nb, tb, d = x_ref.shape
    rows = nb * tb

    def load_rows(ref, r0):
        if nb == 1:
            return ref[0, r0:r0 + sub, :]
        return ref[r0 // tb:(r0 + sub) // tb].reshape(sub, ref.shape[-1])

    def store_rows(ref, r0, lanes, val):
        if nb == 1:
            ref[0, r0:r0 + sub, lanes] = val
        else:
            ref[r0 // tb:(r0 + sub) // tb, :, lanes] = val.reshape(sub // tb, tb, val.shape[-1])

    hns = []
    for r0 in range(0, rows, sub):
        hn = _rms(load_rows(x_ref, r0), g_ref[...]).astype(BF16)
        hns.append(hn)
        qkv = jnp.dot(hn, wqkv_ref[...], preferred_element_type=F32)
        cos = cos_ref[r0:r0 + sub, :]
        sin = sin_ref[r0:r0 + sub, :]

        def rotary(a):
            return a * cos + pltpu.roll(a, HEAD_DIM // 2, 1) * sin

        for h in range(RET_HEADS):
            lanes = slice(h * HEAD_DIM, (h + 1) * HEAD_DIM)
            k = rotary(qkv[:, RET_WIDTH + h * HEAD_DIM:RET_WIDTH + (h + 1) * HEAD_DIM]) * (HEAD_DIM ** -0.5)
            store_rows(q_ref, r0, lanes, rotary(qkv[:, lanes]).astype(BF16))
            store_rows(k_ref, r0, lanes, k.astype(BF16))
            store_rows(kdk_ref, r0, lanes, (k * kd_ref[:, lanes]).astype(BF16))
            store_rows(v_ref, r0, lanes,
                       qkv[:, 2 * RET_WIDTH + h * HEAD_DIM:2 * RET_WIDTH + (h + 1) * HEAD_DIM].astype(BF16))

    if emit_ut:
        nk = rows // TOEP_CHUNK
        hn = jnp.concatenate(hns, axis=0)
        ut = lax.dot_general(wut_ref[...], hn, (((1,), (1,)), ((), ())), preferred_element_type=F32)
        for g in range(SSM_GROUPS):
            grp = slice(g * SSM_GROUP, (g + 1) * SSM_GROUP)
            slabs = jnp.stack([ut[grp, kk * TOEP_CHUNK:(kk + 1) * TOEP_CHUNK] for kk in range(nk)])
            by_channel = jnp.swapaxes(slabs, 0, 1)
            for c in range(SSM_GROUP):
                ut_ref[g, :, c * TOEP_CHUNK:(c + 1) * TOEP_CHUNK] = by_channel[c]


def _front_call(x, norm_g, w_ut, w_qkv, cos, sin, kd_rows, nb, tb, sub, emit_ut):
    b, t, d = x.shape
    rows = nb * tb
    c2 = lambda i, j: (0, 0)
    row_spec = pl.BlockSpec((nb, tb, RET_WIDTH), lambda i, j: (i, j, 0))
    in_specs = [pl.BlockSpec((nb, tb, d), lambda i, j: (i, j, 0)), pl.BlockSpec((1, d), c2)]
    operands = [x, norm_g]
    out_specs, out_shape = [], []
    if emit_ut:
        in_specs.append(pl.BlockSpec((SSM_WIDTH, d), c2))
        operands.append(w_ut)
        out_specs.append(pl.BlockSpec((SSM_GROUPS, None, tb // TOEP_CHUNK, TOEP_LANES), lambda i, j: (0, i, j, 0)))
        out_shape.append(jax.ShapeDtypeStruct((SSM_GROUPS, b, t // TOEP_CHUNK, TOEP_LANES), F32))
    in_specs += [
        pl.BlockSpec((d, 3 * RET_WIDTH), c2),
        pl.BlockSpec((rows, HEAD_DIM), lambda i, j: (j, 0)),
        pl.BlockSpec((rows, HEAD_DIM), lambda i, j: (j, 0)),
        pl.BlockSpec((sub, RET_WIDTH), c2),
    ]
    operands += [w_qkv, cos, sin, kd_rows]
    out_specs += [row_spec] * 4
    out_shape += [jax.ShapeDtypeStruct((b, t, RET_WIDTH), BF16)] * 4
    return pl.pallas_call(
        functools.partial(_front_kernel, emit_ut, sub),
        grid=(b // nb, t // tb),
        in_specs=in_specs,
        out_specs=out_specs,
        out_shape=out_shape,
        compiler_params=pltpu.CompilerParams(
            dimension_semantics=("arbitrary", "arbitrary"), vmem_limit_bytes=VMEM_LIMIT_BYTES),
        name="qkv_rotary_front",
    )(*operands)


def _powers(base_re, base_im, exponent, n_bits):
    pr = jnp.ones_like(base_re)
    pi = jnp.zeros_like(base_re)
    br, bi = base_re, base_im
    for bit in range(n_bits):
        take = ((exponent >> bit) & 1) == 1
        pr, pi = jnp.where(take, pr * br - pi * bi, pr), jnp.where(take, pr * bi + pi * br, pi)
        br, bi = br * br - bi * bi, 2.0 * br * bi
    return pr, pi


def _split_bf16(x):
    hi = x.astype(BF16)
    r = x - hi.astype(F32)
    mid = r.astype(BF16)
    lo = (r - mid.astype(F32)).astype(BF16)
    return hi, mid, lo


def _dot_f32(a, b):
    a3, b3 = _split_bf16(a), _split_bf16(b)
    acc = None
    for i in range(3):
        for j in range(3 - i):
            term = jnp.dot(a3[i], b3[j], preferred_element_type=F32)
            acc = term if acc is None else acc + term
    return acc


def _toeplitz_kernel(u_ref, h0_ref, lcr_ref, lci_ref, lrr_ref, lri_ref, cb_ref, bb1_ref, bb2_ref,
                     ct_ref, d_ref, y_ref, hfin_ref, a_ref, wb_ref, wc_ref, t_ref, ys_ref, yi_ref):
    nb, nk, _ = u_ref.shape
    rows = nb * nk
    L = TOEP_CHUNK
    n_bits = L.bit_length() - 1
    sub_i = lax.broadcasted_iota(jnp.int32, (L, L), 0)
    lane_i = lax.broadcasted_iota(jnp.int32, (L, L), 1)

    lcr, lci = lcr_ref[...], lci_ref[...]
    lane_p = lax.broadcasted_iota(jnp.int32, lcr.shape, 1)
    p0r, p0i = _powers(lcr, lci, lane_p, n_bits)
    p1r, p1i = p0r * lcr - p0i * lci, p0r * lci + p0i * lcr

    taps = _dot_f32(cb_ref[...], jnp.concatenate([p0r, -p0i], axis=0))

    lrr = jnp.broadcast_to(lrr_ref[...], (L, L))
    lri = jnp.broadcast_to(lri_ref[...], (L, L))
    qr, qi = _powers(lrr, lri, (L - 1) - sub_i, n_bits)
    for c in range(SSM_GROUP):
        wb_ref[c * L:(c + 1) * L, :] = (qr * bb1_ref[c:c + 1, :] + qi * bb2_ref[c:c + 1, :]).astype(BF16)
    pa = jnp.concatenate([p1r, p1i], axis=0)
    pb = jnp.concatenate([p1i, p1r], axis=0)
    for c in range(SSM_GROUP):
        wc_ref[:, c * L:(c + 1) * L] = (
            jnp.broadcast_to(ct_ref[:, c:c + 1], (L, L)) * pa
            + jnp.broadcast_to(ct_ref[:, SSM_GROUP + c:SSM_GROUP + c + 1], (L, L)) * pb).astype(BF16)

    a_ref[...] = u_ref[...].reshape(rows, TOEP_LANES).astype(BF16)

    s_in = jnp.dot(a_ref[...], wb_ref[...], preferred_element_type=F32)
    s_in = jnp.swapaxes(s_in.reshape(nb, nk, L), 0, 1)
    l128r = qr[0:1] * lrr[0:1] - qi[0:1] * lri[0:1]
    l128i = qr[0:1] * lri[0:1] + qi[0:1] * lrr[0:1]
    half = L // 2
    is_re = lane_i[0:1] < half
    mul_same = jnp.broadcast_to(l128r, (nb, L))
    mul_swap = jnp.broadcast_to(jnp.where(is_re, -l128i, l128i), (nb, L))
    s_sw = pltpu.roll(s_in.reshape(nk * nb, L), half, 1).reshape(nk, nb, L)
    h = h0_ref[...]
    h_sw = pltpu.roll(h, half, 1)
    h_prev = []
    for k in range(nk):
        h_prev.append(h)
        h, h_sw = (h * mul_same + h_sw * mul_swap + s_in[k],
                   h_sw * mul_same - h * mul_swap + s_sw[k])
    hfin_ref[...] = h
    h_prev = jnp.swapaxes(jnp.stack(h_prev), 0, 1).reshape(rows, L).astype(BF16)

    causal = lane_i >= sub_i
    for pair in range(SSM_GROUP // 2):
        slot = pair
        for ci in range(SSM_GROUP):
            for cc in range(2):
                r = ci * SSM_GROUP + 2 * pair + cc
                tap = jnp.broadcast_to(taps[r:r + 1, :], (L, L))
                toep = jnp.where(causal, pltpu.roll(tap, 0, 1, stride=1, stride_axis=0), 0.0)
                t_ref[slot, ci * L:(ci + 1) * L, cc * L:(cc + 1) * L] = toep.astype(BF16)
        yv = jnp.dot(a_ref[...], t_ref[slot], preferred_element_type=F32)
        for cc in range(2):
            c = 2 * pair + cc
            uc = u_ref[:, :, c * L:(c + 1) * L]
            ys_ref[c] = yv[:, cc * L:(cc + 1) * L].reshape(nb, nk, L) + d_ref[c:c + 1, :] * uc
    inter = jnp.dot(h_prev, wc_ref[...], preferred_element_type=F32)
    for c in range(SSM_GROUP):
        yi_ref[c] = inter[:, c * L:(c + 1) * L].reshape(nb, nk, L)
    for b in range(nb):
        y_ref[b] = jnp.swapaxes(ys_ref[:, b] + yi_ref[:, b], 0, 1)


def _toeplitz_call(ut, h0, tabs):
    g, nb, nk, lanes = ut.shape
    L = TOEP_CHUNK
    rows = nb * nk
    lcr, lci, lrr, lri, cb, bb1, bb2, ct, dtab = tabs

    def per_group(*shape):
        return pl.BlockSpec((None,) + shape, lambda i: (i,) + (0,) * len(shape))

    return pl.pallas_call(
        _toeplitz_kernel,
        grid=(g,),
        in_specs=[
            per_group(nb, nk, lanes), per_group(nb, L),
            per_group(SSM_STATE, L), per_group(SSM_STATE, L), per_group(1, L), per_group(1, L),
            per_group(SSM_GROUP * SSM_GROUP, L), per_group(SSM_GROUP, L), per_group(SSM_GROUP, L),
            per_group(L, 2 * SSM_GROUP), per_group(SSM_GROUP, L),
        ],
        out_specs=[per_group(nb, nk, SSM_GROUP, L), per_group(nb, L)],
        out_shape=[
            jax.ShapeDtypeStruct((g, nb, nk, SSM_GROUP, L), F32),
            jax.ShapeDtypeStruct((g, nb, L), F32),
        ],
        scratch_shapes=[
            pltpu.VMEM((rows, lanes), BF16),
            pltpu.VMEM((lanes, L), BF16),
            pltpu.VMEM((L, lanes), BF16),
            pltpu.VMEM((SSM_GROUP // 2, lanes, 2 * L), BF16),
            pltpu.VMEM((SSM_GROUP, nb, nk, L), F32),
            pltpu.VMEM((SSM_GROUP, nb, nk, L), F32),
        ],
        compiler_params=pltpu.CompilerParams(
            dimension_semantics=("arbitrary",), vmem_limit_bytes=VMEM_LIMIT_BYTES),
        name="ssm_toeplitz",
    )(ut, h0, lcr, lci, lrr, lri, cb, bb1, bb2, ct, dtab)


def _layer_kernel(chunk, sub, y_by_group, x_ref, y_ref, q_ref, k_ref, kdk_ref, v_ref, s0_ref, ng_ref,
                  wg_ref, rg_ref, dm_ref, qd_ref, sd_ref, wglu_ref, bglu_ref, wo_ref, fg_ref,
                  out_ref, sfin_ref, o_ref):
    nb, tb, d = x_ref.shape
    rows = nb * tb

    @pl.when(pl.program_id(1) == 0)
    def _():
        sfin_ref[...] = s0_ref[...]

    def load_rows(ref, r0, lanes=slice(None)):
        if nb == 1:
            return ref[0, r0:r0 + sub, lanes]
        val = ref[r0 // tb:(r0 + sub) // tb, :, lanes]
        return val.reshape(sub, val.shape[-1])

    def store_rows(ref, r0, val):
        if nb == 1:
            ref[0, r0:r0 + sub, :] = val
        else:
            ref[r0 // tb:(r0 + sub) // tb] = val.reshape(sub // tb, tb, ref.shape[-1])

    rg = rg_ref[...]
    for r0 in range(0, rows, sub):
        x = load_rows(x_ref, r0)
        hn = _rms(x, ng_ref[...]).astype(BF16)
        gates = jnp.dot(hn, wg_ref[...], preferred_element_type=F32)

        for h in range(RET_HEADS):
            lanes = slice(h * HEAD_DIM, (h + 1) * HEAD_DIM)
            q, kb, kdk, vb = (load_rows(ref, r0, lanes) for ref in (q_ref, k_ref, kdk_ref, v_ref))
            dm = dm_ref[h]
            qd = qd_ref[h]
            sd = sd_ref[h]
            for c0 in range(0, sub, chunk):
                n = (r0 + c0) // tb
                r = slice(c0, c0 + chunk)
                qc, kc, vc = q[r], kb[r], vb[r]
                s = sfin_ref[n, h]
                scores = lax.dot_general(qc, kc, (((1,), (1,)), ((), ())), preferred_element_type=F32)
                p = (scores * dm).astype(BF16)
                o = jnp.dot(p, vc, preferred_element_type=F32)
                o = o + jnp.dot(qc, s.astype(BF16), preferred_element_type=F32) * qd
                sfin_ref[n, h] = sd * s + jnp.dot(kdk[r].T, vc, preferred_element_type=F32)
                o_ref[r0 + c0:r0 + c0 + chunk, lanes] = o

        g_ret = gates[:, :RET_WIDTH]
        g_ssm = gates[:, RET_WIDTH:]
        o_parts = []
        for h in range(RET_HEADS):
            lanes = slice(h * HEAD_DIM, (h + 1) * HEAD_DIM)
            o_parts.append(_rms(o_ref[r0:r0 + sub, lanes], rg[:, lanes]))
        o = jnp.concatenate(o_parts, axis=-1) * jax.nn.silu(g_ret)

        if y_by_group:
            k0 = r0 // TOEP_CHUNK
            y = jnp.concatenate(
                [y_ref[:, k0 + j].reshape(SSM_WIDTH, TOEP_CHUNK).T for j in range(sub // TOEP_CHUNK)], axis=0)
        else:
            y = load_rows(y_ref, r0)
        z = jax.nn.gelu(y)
        gate = jnp.dot(z.astype(BF16), wglu_ref[...], preferred_element_type=F32) + bglu_ref[...]
        z = z * jax.nn.sigmoid(gate)
        z = z * jax.nn.silu(g_ssm)

        mix = jnp.concatenate([o, z], axis=-1).astype(BF16)
        res = x + jnp.dot(mix, wo_ref[...], preferred_element_type=F32)
        store_rows(out_ref, r0, _rms(res, fg_ref[...]))


def _layer_call(x, y, q, k, kdk, v, s0, norm_g, w_gates, ret_norm_g, dm, qd, sd, w_glu, b_glu,
                w_out, final_g, plan):
    b, t, d = x.shape
    nb, tb, chunk, sub = plan.nb, plan.tb, plan.chunk, plan.sub
    rows = nb * tb
    c2 = lambda i, j: (0, 0)
    c3 = lambda i, j: (0, 0, 0)
    state_spec = pl.BlockSpec((nb, RET_HEADS, HEAD_DIM, HEAD_DIM), lambda i, j: (i, 0, 0, 0))
    row_spec = pl.BlockSpec((nb, tb, RET_WIDTH), lambda i, j: (i, j, 0))
    if plan.toeplitz:
        y_spec = pl.BlockSpec((SSM_GROUPS, None, tb // TOEP_CHUNK, SSM_GROUP, TOEP_CHUNK),
                              lambda i, j: (0, i, j, 0, 0))
    else:
        y_spec = pl.BlockSpec((nb, tb, SSM_WIDTH), lambda i, j: (i, j, 0))
    return pl.pallas_call(
        functools.partial(_layer_kernel, chunk, sub, plan.toeplitz),
        grid=(b // nb, t // tb),
        in_specs=[
            pl.BlockSpec((nb, tb, d), lambda i, j: (i, j, 0)),
            y_spec,
            row_spec, row_spec, row_spec, row_spec,
            state_spec,
            pl.BlockSpec((1, d), c2),
            pl.BlockSpec((d, RET_WIDTH + SSM_WIDTH), c2),
            pl.BlockSpec((1, RET_WIDTH), c2),
            pl.BlockSpec((RET_HEADS, chunk, chunk), c3),
            pl.BlockSpec((RET_HEADS, chunk, HEAD_DIM), c3),
            pl.BlockSpec((RET_HEADS, HEAD_DIM, HEAD_DIM), c3),
            pl.BlockSpec((SSM_WIDTH, SSM_WIDTH), c2),
            pl.BlockSpec((1, SSM_WIDTH), c2),
            pl.BlockSpec((D_MODEL, D_MODEL), c2),
            pl.BlockSpec((1, d), c2),
        ],
        out_specs=[
            pl.BlockSpec((nb, tb, d), lambda i, j: (i, j, 0)),
            state_spec,
        ],
        out_shape=[
            jax.ShapeDtypeStruct((b, t, d), F32),
            jax.ShapeDtypeStruct((b, RET_HEADS, HEAD_DIM, HEAD_DIM), F32),
        ],
        scratch_shapes=[pltpu.VMEM((rows, RET_WIDTH), F32)],
        compiler_params=pltpu.CompilerParams(
            dimension_semantics=("arbitrary", "arbitrary"), vmem_limit_bytes=VMEM_LIMIT_BYTES),
        name="retention_glu_layer",
    )(x, y, q, k, kdk, v, s0, norm_g, w_gates, ret_norm_g, dm, qd, sd, w_glu, b_glu, w_out, final_g)


def _rotary_tables(pos0, steps, reps):
    f = np.float32
    half = HEAD_DIM // 2
    inv = (f(ROPE_BASE) ** (-np.arange(half, dtype=f) / f(half))).astype(f)
    ang = (f(pos0) + np.arange(steps, dtype=f))[:, None] * inv[None, :]
    cos, sin = np.cos(ang), np.sin(ang)
    cos2 = np.concatenate([cos, cos], axis=-1)
    sin2 = np.concatenate([-sin, sin], axis=-1)
    return jnp.asarray(np.tile(cos2, (reps, 1)), F32), jnp.asarray(np.tile(sin2, (reps, 1)), F32)


def _retention_tables(chunk, sub):
    f = np.float32
    log_g = np.log1p(-np.power(f(2.0), f(-5.0) - np.arange(RET_HEADS, dtype=f))).astype(f)
    idx = np.arange(chunk, dtype=f)
    diff = idx[:, None] - idx[None, :]
    causal = diff >= 0
    dm = np.where(causal[None], np.exp(np.where(causal, diff, f(0.0))[None] * log_g[:, None, None]), f(0.0))
    q_decay = np.exp((idx + f(1.0))[None, :] * log_g[:, None])
    k_decay = np.exp((f(chunk) - f(1.0) - idx)[None, :] * log_g[:, None])
    s_decay = np.exp(f(chunk) * log_g)
    qd = np.broadcast_to(q_decay[:, :, None], (RET_HEADS, chunk, HEAD_DIM))
    sd = np.broadcast_to(s_decay[:, None, None], (RET_HEADS, HEAD_DIM, HEAD_DIM))
    kd = np.broadcast_to(k_decay.T[:, :, None], (chunk, RET_HEADS, HEAD_DIM)).reshape(chunk, RET_WIDTH)
    kd_rows = np.tile(kd, (sub // chunk, 1))
    return tuple(jnp.asarray(np.ascontiguousarray(a), F32) for a in (dm, qd, sd, kd_rows))


def _discretise(lambda_re, lambda_im, log_step, b_re, b_im):
    lam = lax.complex(jnp.minimum(lambda_re.astype(F32), LAMBDA_RE_MAX), lambda_im.astype(F32))
    dt = jnp.exp(log_step.astype(F32))[:, None]
    lam_bar = jnp.exp(lam * dt)
    b_bar = ((lam_bar - 1.0) / lam)[..., None] * lax.complex(b_re.astype(F32), b_im.astype(F32))
    return lam_bar, b_bar


def _scan_tables(lam_bar, b_bar, c_re, c_im):
    gl = SSM_GROUPS // SSM_HALVES
    eye = jnp.eye(gl, dtype=F32)
    ar = jnp.real(lam_bar).reshape(SSM_HALVES, HALF_RI)
    ai = jnp.imag(lam_bar).reshape(SSM_HALVES, HALF_RI)

    def place_b(m):
        m = m.reshape(SSM_HALVES, gl, SSM_STATE, SSM_GROUP)
        return jnp.einsum("hgpc,gk->hgckp", m, eye).reshape(SSM_HALVES, HALF_IN, HALF_RI)

    def place_c(m):
        m = m.reshape(SSM_HALVES, gl, SSM_GROUP, SSM_STATE)
        return jnp.einsum("hgcp,gk->hkpgc", m, eye).reshape(SSM_HALVES, HALF_RI, HALF_IN)

    bd = jnp.concatenate([place_b(jnp.real(b_bar)), place_b(jnp.imag(b_bar))], axis=-1).astype(BF16)
    cd = jnp.concatenate([place_c(c_re.astype(F32)), place_c(-c_im.astype(F32))], axis=1).astype(BF16)
    return ar, ai, bd, cd


def _toeplitz_tables(lam_bar, b_bar, c_re, c_im, d):
    g, p = lam_bar.shape
    L = TOEP_CHUNK
    lre, lim = jnp.real(lam_bar), jnp.imag(lam_bar)
    bre, bim = jnp.real(b_bar), jnp.imag(b_bar)
    cre, cim = c_re.astype(F32), c_im.astype(F32)
    lcr = jnp.broadcast_to(lre[:, :, None], (g, p, L))
    lci = jnp.broadcast_to(lim[:, :, None], (g, p, L))
    lrr = jnp.concatenate([lre, lre], axis=-1)[:, None, :]
    lri = jnp.concatenate([lim, lim], axis=-1)[:, None, :]
    cb_re = cre[:, None, :, :] * bre.transpose(0, 2, 1)[:, :, None, :] - cim[:, None, :, :] * bim.transpose(0, 2, 1)[:, :, None, :]
    cb_im = cre[:, None, :, :] * bim.transpose(0, 2, 1)[:, :, None, :] + cim[:, None, :, :] * bre.transpose(0, 2, 1)[:, :, None, :]
    cb = jnp.concatenate([cb_re, cb_im], axis=-1).reshape(g, SSM_GROUP * SSM_GROUP, 2 * p)
    bt_re, bt_im = bre.transpose(0, 2, 1), bim.transpose(0, 2, 1)
    bb1 = jnp.concatenate([bt_re, bt_im], axis=-1)
    bb2 = jnp.concatenate([-bt_im, bt_re], axis=-1)
    ct_re, ct_im = cre.transpose(0, 2, 1), cim.transpose(0, 2, 1)
    col_a = jnp.concatenate([ct_re, -ct_re], axis=1)
    col_b = jnp.concatenate([-ct_im, -ct_im], axis=1)
    ct = jnp.concatenate([col_a, col_b], axis=-1)
    dtab = jnp.broadcast_to(d.astype(F32).reshape(g, SSM_GROUP, 1), (g, SSM_GROUP, L))
    return lcr, lci, lrr, lri, cb, bb1, bb2, ct, dtab


def _state_to_lanes(re, im):
    b = re.shape[0]
    re = re.astype(F32).reshape(b, SSM_HALVES, HALF_RI)
    im = im.astype(F32).reshape(b, SSM_HALVES, HALF_RI)
    return jnp.concatenate([re, im], axis=-1).reshape(b, N_STATE)


def _lanes_to_state(h):
    b = h.shape[0]
    h = h.reshape(b, SSM_HALVES, 2, HALF_RI)
    return (h[:, :, 0].reshape(b, SSM_GROUPS, SSM_STATE), h[:, :, 1].reshape(b, SSM_GROUPS, SSM_STATE))


class _Plan(NamedTuple):
    nb: int
    tb: int
    chunk: int
    sub: int
    toeplitz: bool


def _plan(b, t):
    if t >= LAYER_ROWS:
        assert t % LAYER_ROWS == 0 and LAYER_ROWS % LAYER_CHUNK == 0 and LAYER_CHUNK % TOEP_CHUNK == 0
        return _Plan(nb=1, tb=LAYER_ROWS, chunk=LAYER_CHUNK, sub=LAYER_CHUNK, toeplitz=True)
    nb = min(b, LAYER_ROWS // t)
    assert b % nb == 0
    return _Plan(nb=nb, tb=t, chunk=t, sub=nb * t, toeplitz=False)


def _stream_step(x, pos0, ret_s0, h0_re, h0_im, weights):
    (norm_g, w_qkv, w_gates, w_u, ret_norm_g, lam_bar, b_bar, c_re, c_im, ssm_d, w_glu, b_glu, w_out,
     final_g) = weights
    b, t, _ = x.shape
    plan = _plan(b, t)
    cos, sin = _rotary_tables(pos0, t, plan.nb)
    dm, qd, sd, kd_rows = _retention_tables(plan.chunk, plan.sub)
    front = _front_call(x, norm_g, w_u.T if plan.toeplitz else None, w_qkv, cos, sin, kd_rows,
                        plan.nb, plan.tb, plan.sub, plan.toeplitz)
    if plan.toeplitz:
        ut, q, k, kdk, v = front
        h0 = jnp.concatenate([h0_re, h0_im], axis=-1).astype(F32).transpose(1, 0, 2)
        y, h_fin = _toeplitz_call(ut, h0, _toeplitz_tables(lam_bar, b_bar, c_re, c_im, ssm_d))
        h_fin = h_fin.transpose(1, 0, 2)
        h_re, h_im = h_fin[..., :SSM_STATE], h_fin[..., SSM_STATE:]
    else:
        q, k, kdk, v = front
        ar, ai, bd, cd = _scan_tables(lam_bar, b_bar, c_re, c_im)
        y, h_fin = _ssm_call(x, norm_g, w_u, _state_to_lanes(h0_re, h0_im), ar, ai, bd, cd,
                             ssm_d[None, :].astype(F32), plan.tb)
        h_re, h_im = _lanes_to_state(h_fin)
    out, s_fin = _layer_call(x, y, q, k, kdk, v, ret_s0, norm_g, w_gates, ret_norm_g, dm, qd, sd,
                             w_glu, b_glu, w_out, final_g, plan)
    return out, s_fin, h_re, h_im


def kernel(x_prompt, x_sample, state_ret, state_ssm_re, state_ssm_im, norm_g, w_in, ret_norm_g,
           ssm_lambda_re, ssm_lambda_im, ssm_log_step, ssm_b_re, ssm_b_im, ssm_c_re, ssm_c_im,
           ssm_d, w_glu, b_glu, w_out, final_norm_g):
    depth = w_in.shape[0]
    assert depth == 1, "single-layer trunk"
    bp, tp, _ = x_prompt.shape
    bs, ts, _ = x_sample.shape
    l = 0
    w = w_in[l]
    u_lo = 4 * RET_WIDTH
    w_qkv = w[:, :3 * RET_WIDTH].astype(BF16)
    w_gates = jnp.concatenate([w[:, 3 * RET_WIDTH:u_lo], w[:, u_lo + SSM_WIDTH:]], axis=1).astype(BF16)
    w_u = w[:, u_lo:u_lo + SSM_WIDTH].astype(BF16)
    lam_bar, b_bar = _discretise(ssm_lambda_re[l], ssm_lambda_im[l], ssm_log_step[l], ssm_b_re[l], ssm_b_im[l])
    weights = (
        norm_g[l][None, :].astype(F32), w_qkv, w_gates, w_u, ret_norm_g[l].reshape(1, RET_WIDTH).astype(F32),
        lam_bar, b_bar, ssm_c_re[l], ssm_c_im[l], ssm_d[l],
        w_glu[l].astype(BF16), b_glu[l][None, :].astype(F32),
        w_out[l].astype(BF16), final_norm_g[None, :].astype(F32),
    )

    ret0_p = jnp.zeros((bp, RET_HEADS, HEAD_DIM, HEAD_DIM), F32)
    h0_p = jnp.zeros((bp, SSM_GROUPS, SSM_STATE), F32)
    y_p, r_p, hp_re, hp_im = _stream_step(x_prompt, 0, ret0_p, h0_p, h0_p, weights)
    y_s, r_s, hs_re, hs_im = _stream_step(x_sample, PAST_LEN, state_ret[l].astype(F32),
                                          state_ssm_re[l], state_ssm_im[l], weights)
    return (y_p, y_s, r_p[None], hp_re[None], hp_im[None], r_s[None], hs_re[None], hs_im[None])
```

```python
import functools
from typing import NamedTuple

import jax
import jax.numpy as jnp
import numpy as np
from jax import lax
from jax.experimental import pallas as pl
from jax.experimental.pallas import tpu as pltpu

D_MODEL = 1024
RET_HEADS = 4
HEAD_DIM = 128
RET_WIDTH = RET_HEADS * HEAD_DIM
SSM_WIDTH = 512
SSM_GROUP = 16
SSM_GROUPS = 32
SSM_STATE = 64
ROPE_BASE = 10000.0
EPS = 1e-6
LAMBDA_RE_MAX = -1e-4
PAST_LEN = 2048

LAYER_ROWS = 1024
LAYER_CHUNK = 256
TOEP_CHUNK = 128
TOEP_LANES = SSM_GROUP * TOEP_CHUNK

SSM_HALVES = 2
HALF_IN = SSM_WIDTH // SSM_HALVES
HALF_RI = (SSM_GROUPS // SSM_HALVES) * SSM_STATE
HALF_STATE = 2 * HALF_RI
N_STATE = SSM_HALVES * HALF_STATE
SCAN_LANES = 512

VMEM_LIMIT_BYTES = 56 * 1024 * 1024

F32 = jnp.float32
BF16 = jnp.bfloat16


def _rms(x, g):
    return x * lax.rsqrt(jnp.mean(x * x, axis=-1, keepdims=True) + EPS) * g


def _ssm_kernel(x_ref, g_ref, wu_ref, h0_ref, ar_ref, ai_ref, bd_ref, cd_ref, d_ref, y_ref, hfin_ref,
                ubt_ref, utb_ref, bu_ref, hs_ref, ytb_ref):
    nb, tb, d_model = x_ref.shape
    w = SSM_WIDTH
    rows = nb * tb

    @pl.when(pl.program_id(0) == 0)
    def _():
        hfin_ref[...] = h0_ref[...]

    hn = _rms(x_ref[...].reshape(rows, d_model), g_ref[...]).astype(BF16)
    ubt_ref[...] = jnp.dot(hn, wu_ref[...], preferred_element_type=F32).reshape(nb, tb, w)
    for t in range(tb):
        utb_ref[t] = ubt_ref[:, t, :]
    u = utb_ref[...].reshape(rows, w)
    ub = u.astype(BF16)

    for h in range(SSM_HALVES):
        cols = slice(h * HALF_IN, (h + 1) * HALF_IN)
        yh = d_ref[:, cols] * u[:, cols]
        for q in range(HALF_RI // SCAN_LANES):
            re_h = q * SCAN_LANES
            im_h = HALF_RI + re_h
            re = h * HALF_STATE + re_h
            im = h * HALF_STATE + im_h
            bu_ref[:, re:re + SCAN_LANES] = jnp.dot(
                ub[:, cols], bd_ref[h, :, re_h:re_h + SCAN_LANES], preferred_element_type=F32)
            bu_ref[:, im:im + SCAN_LANES] = jnp.dot(
                ub[:, cols], bd_ref[h, :, im_h:im_h + SCAN_LANES], preferred_element_type=F32)
            ar = jnp.broadcast_to(ar_ref[h:h + 1, re_h:re_h + SCAN_LANES], (nb, SCAN_LANES))
            ai = jnp.broadcast_to(ai_ref[h:h + 1, re_h:re_h + SCAN_LANES], (nb, SCAN_LANES))
            hr = hfin_ref[:, re:re + SCAN_LANES]
            hi = hfin_ref[:, im:im + SCAN_LANES]
            for t in range(tb):
                r = slice(t * nb, (t + 1) * nb)
                hr, hi = (ar * hr - ai * hi + bu_ref[r, re:re + SCAN_LANES],
                          ar * hi + ai * hr + bu_ref[r, im:im + SCAN_LANES])
                hs_ref[r, re:re + SCAN_LANES] = hr.astype(BF16)
                hs_ref[r, im:im + SCAN_LANES] = hi.astype(BF16)
            hfin_ref[:, re:re + SCAN_LANES] = hr
            hfin_ref[:, im:im + SCAN_LANES] = hi
            yh = yh + jnp.dot(hs_ref[:, re:re + SCAN_LANES], cd_ref[h, re_h:re_h + SCAN_LANES, :],
                              preferred_element_type=F32)
            yh = yh + jnp.dot(hs_ref[:, im:im + SCAN_LANES], cd_ref[h, im_h:im_h + SCAN_LANES, :],
                              preferred_element_type=F32)
        ytb_ref[:, :, cols] = yh.reshape(tb, nb, HALF_IN)
    for b in range(nb):
        y_ref[b] = ytb_ref[:, b, :]


def _ssm_call(x, norm_g, w_u, h0, ar, ai, bd, cd, d, tb):
    nb, t, d_model = x.shape
    w = SSM_WIDTH
    rows = nb * tb
    n_blocks = t // tb
    c2 = lambda i: (0, 0)
    c3 = lambda i: (0, 0, 0)
    return pl.pallas_call(
        _ssm_kernel,
        grid=(n_blocks,),
        in_specs=[
            pl.BlockSpec((nb, tb, d_model), lambda i: (0, i, 0)),
            pl.BlockSpec((1, d_model), c2),
            pl.BlockSpec((d_model, w), c2),
            pl.BlockSpec((nb, N_STATE), c2),
            pl.BlockSpec((SSM_HALVES, HALF_RI), c2),
            pl.BlockSpec((SSM_HALVES, HALF_RI), c2),
            pl.BlockSpec((SSM_HALVES, HALF_IN, HALF_STATE), c3),
            pl.BlockSpec((SSM_HALVES, HALF_STATE, HALF_IN), c3),
            pl.BlockSpec((1, w), c2),
        ],
        out_specs=[
            pl.BlockSpec((nb, tb, w), lambda i: (0, i, 0)),
            pl.BlockSpec((nb, N_STATE), c2),
        ],
        out_shape=[
            jax.ShapeDtypeStruct((nb, t, w), F32),
            jax.ShapeDtypeStruct((nb, N_STATE), F32),
        ],
        scratch_shapes=[
            pltpu.VMEM((nb, tb, w), F32),
            pltpu.VMEM((tb, nb, w), F32),
            pltpu.VMEM((rows, N_STATE), F32),
            pltpu.VMEM((rows, N_STATE), BF16),
            pltpu.VMEM((tb, nb, w), F32),
        ],
        compiler_params=pltpu.CompilerParams(
            dimension_semantics=("arbitrary",), vmem_limit_bytes=VMEM_LIMIT_BYTES),
        name="ssm_scan",
    )(x, norm_g, w_u, h0, ar, ai, bd, cd, d)


def _front_kernel(emit_ut, sub, x_ref, g_ref, *refs):
    if emit_ut:
        wut_ref, wqkv_ref, cos_ref, sin_ref, kd_ref, ut_ref, q_ref, k_ref, kdk_ref, v_ref = refs
    else:
        wqkv_ref, cos_ref, sin_ref, kd_ref, q_ref, k_ref, kdk_ref, v_ref = refs
    nb, tb, d = x_ref.shape
    rows = nb * tb

    def load_rows(ref, r0):
        if nb == 1:
            return ref[0, r0:r0 + sub, :]
        return ref[r0 // tb:(r0 + sub) // tb].reshape(sub, ref.shape[-1])

    def store_rows(ref, r0, lanes, val):
        if nb == 1:
            ref[0, r0:r0 + sub, lanes] = val
        else:
            ref[r0 // tb:(r0 + sub) // tb, :, lanes] = val.reshape(sub // tb, tb, val.shape[-1])

    hns = []
    for r0 in range(0, rows, sub):
        hn = _rms(load_rows(x_ref, r0), g_ref[...]).astype(BF16)
        hns.append(hn)
        qkv = jnp.dot(hn, wqkv_ref[...], preferred_element_type=F32)
        cos = cos_ref[r0:r0 + sub, :]
        sin = sin_ref[r0:r0 + sub, :]

        def rotary(a):
            return a * cos + pltpu.roll(a, HEAD_DIM // 2, 1) * sin

        for h in range(RET_HEADS):
            lanes = slice(h * HEAD_DIM, (h + 1) * HEAD_DIM)
            k = rotary(qkv[:, RET_WIDTH + h * HEAD_DIM:RET_WIDTH + (h + 1) * HEAD_DIM]) * (HEAD_DIM ** -0.5)
            store_rows(q_ref, r0, lanes, rotary(qkv[:, lanes]).astype(BF16))
            store_rows(k_ref, r0, lanes, k.astype(BF16))
            store_rows(kdk_ref, r0, lanes, (k * kd_ref[:, lanes]).astype(BF16))
            store_rows(v_ref, r0, lanes,
                       qkv[:, 2 * RET_WIDTH + h * HEAD_DIM:2 * RET_WIDTH + (h + 1) * HEAD_DIM].astype(BF16))

    if emit_ut:
        nk = rows // TOEP_CHUNK
        hn = jnp.concatenate(hns, axis=0)
        ut = lax.dot_general(wut_ref[...], hn, (((1,), (1,)), ((), ())), preferred_element_type=F32)
        for g in range(SSM_GROUPS):
            grp = slice(g * SSM_GROUP, (g + 1) * SSM_GROUP)
            slabs = jnp.stack([ut[grp, kk * TOEP_CHUNK:(kk + 1) * TOEP_CHUNK] for kk in range(nk)])
            by_channel = jnp.swapaxes(slabs, 0, 1)
            for c in range(SSM_GROUP):
                ut_ref[g, :, c * TOEP_CHUNK:(c + 1) * TOEP_CHUNK] = by_channel[c]


def _front_call(x, norm_g, w_ut, w_qkv, cos, sin, kd_rows, nb, tb, sub, emit_ut):
    b, t, d = x.shape
    rows = nb * tb
    c2 = lambda i, j: (0, 0)
    row_spec = pl.BlockSpec((nb, tb, RET_WIDTH), lambda i, j: (i, j, 0))
    in_specs = [pl.BlockSpec((nb, tb, d), lambda i, j: (i, j, 0)), pl.BlockSpec((1, d), c2)]
    operands = [x, norm_g]
    out_specs, out_shape = [], []
    if emit_ut:
        in_specs.append(pl.BlockSpec((SSM_WIDTH, d), c2))
        operands.append(w_ut)
        out_specs.append(pl.BlockSpec((SSM_GROUPS, None, tb // TOEP_CHUNK, TOEP_LANES), lambda i, j: (0, i, j, 0)))
        out_shape.append(jax.ShapeDtypeStruct((SSM_GROUPS, b, t // TOEP_CHUNK, TOEP_LANES), F32))
    in_specs += [
        pl.BlockSpec((d, 3 * RET_WIDTH), c2),
        pl.BlockSpec((rows, HEAD_DIM), lambda i, j: (j, 0)),
        pl.BlockSpec((rows, HEAD_DIM), lambda i, j: (j, 0)),
        pl.BlockSpec((sub, RET_WIDTH), c2),
    ]
    operands += [w_qkv, cos, sin, kd_rows]
    out_specs += [row_spec] * 4
    out_shape += [jax.ShapeDtypeStruct((b, t, RET_WIDTH), BF16)] * 4
    return pl.pallas_call(
        functools.partial(_front_kernel, emit_ut, sub),
        grid=(b // nb, t // tb),
        in_specs=in_specs,
        out_specs=out_specs,
        out_shape=out_shape,
        compiler_params=pltpu.CompilerParams(
            dimension_semantics=("arbitrary", "arbitrary"), vmem_limit_bytes=VMEM_LIMIT_BYTES),
        name="qkv_rotary_front",
    )(*operands)


def _powers(base_re, base_im, exponent, n_bits):
    pr = jnp.ones_like(base_re)
    pi = jnp.zeros_like(base_re)
    br, bi = base_re, base_im
    for bit in range(n_bits):
        take = ((exponent >> bit) & 1) == 1
        pr, pi = jnp.where(take, pr * br - pi * bi, pr), jnp.where(take, pr * bi + pi * br, pi)
        br, bi = br * br - bi * bi, 2.0 * br * bi
    return pr, pi


def _split_bf16(x):
    hi = x.astype(BF16)
    r = x - hi.astype(F32)
    mid = r.astype(BF16)
    lo = (r - mid.astype(F32)).astype(BF16)
    return hi, mid, lo


def _dot_f32(a, b):
    a3, b3 = _split_bf16(a), _split_bf16(b)
    acc = None
    for i in range(3):
        for j in range(3 - i):
            term = jnp.dot(a3[i], b3[j], preferred_element_type=F32)
            acc = term if acc is None else acc + term
    return acc


def _toeplitz_kernel(u_ref, h0_ref, lcr_ref, lci_ref, lrr_ref, lri_ref, cb_ref, bb1_ref, bb2_ref,
                     ct_ref, d_ref, y_ref, hfin_ref, a_ref, wb_ref, wc_ref, t_ref, ys_ref, yi_ref):
    nb, nk, _ = u_ref.shape
    rows = nb * nk
    L = TOEP_CHUNK
    n_bits = L.bit_length() - 1
    sub_i = lax.broadcasted_iota(jnp.int32, (L, L), 0)
    lane_i = lax.broadcasted_iota(jnp.int32, (L, L), 1)

    lcr, lci = lcr_ref[...], lci_ref[...]
    lane_p = lax.broadcasted_iota(jnp.int32, lcr.shape, 1)
    p0r, p0i = _powers(lcr, lci, lane_p, n_bits)
    p1r, p1i = p0r * lcr - p0i * lci, p0r * lci + p0i * lcr

    taps = _dot_f32(cb_ref[...], jnp.concatenate([p0r, -p0i], axis=0))

    lrr = jnp.broadcast_to(lrr_ref[...], (L, L))
    lri = jnp.broadcast_to(lri_ref[...], (L, L))
    qr, qi = _powers(lrr, lri, (L - 1) - sub_i, n_bits)
    for c in range(SSM_GROUP):
        wb_ref[c * L:(c + 1) * L, :] = (qr * bb1_ref[c:c + 1, :] + qi * bb2_ref[c:c + 1, :]).astype(BF16)
    pa = jnp.concatenate([p1r, p1i], axis=0)
    pb = jnp.concatenate([p1i, p1r], axis=0)
    for c in range(SSM_GROUP):
        wc_ref[:, c * L:(c + 1) * L] = (
            jnp.broadcast_to(ct_ref[:, c:c + 1], (L, L)) * pa
            + jnp.broadcast_to(ct_ref[:, SSM_GROUP + c:SSM_GROUP + c + 1], (L, L)) * pb).astype(BF16)

    a_ref[...] = u_ref[...].reshape(rows, TOEP_LANES).astype(BF16)

    s_in = jnp.dot(a_ref[...], wb_ref[...], preferred_element_type=F32)
    s_in = jnp.swapaxes(s_in.reshape(nb, nk, L), 0, 1)
    l128r = qr[0:1] * lrr[0:1] - qi[0:1] * lri[0:1]
    l128i = qr[0:1] * lri[0:1] + qi[0:1] * lrr[0:1]
    half = L // 2
    is_re = lane_i[0:1] < half
    mul_same = jnp.broadcast_to(l128r, (nb, L))
    mul_swap = jnp.broadcast_to(jnp.where(is_re, -l128i, l128i), (nb, L))
    s_sw = pltpu.roll(s_in.reshape(nk * nb, L), half, 1).reshape(nk, nb, L)
    h = h0_ref[...]
    h_sw = pltpu.roll(h, half, 1)
    h_prev = []
    for k in range(nk):
        h_prev.append(h)
        h, h_sw = (h * mul_same + h_sw * mul_swap + s_in[k],
                   h_sw * mul_same - h * mul_swap + s_sw[k])
    hfin_ref[...] = h
    h_prev = jnp.swapaxes(jnp.stack(h_prev), 0, 1).reshape(rows, L).astype(BF16)

    causal = lane_i >= sub_i
    for pair in range(SSM_GROUP // 2):
        slot = pair
        for ci in range(SSM_GROUP):
            for cc in range(2):
                r = ci * SSM_GROUP + 2 * pair + cc
                tap = jnp.broadcast_to(taps[r:r + 1, :], (L, L))
                toep = jnp.where(causal, pltpu.roll(tap, 0, 1, stride=1, stride_axis=0), 0.0)
                t_ref[slot, ci * L:(ci + 1) * L, cc * L:(cc + 1) * L] = toep.astype(BF16)
        yv = jnp.dot(a_ref[...], t_ref[slot], preferred_element_type=F32)
        for cc in range(2):
            c = 2 * pair + cc
            uc = u_ref[:, :, c * L:(c + 1) * L]
            ys_ref[c] = yv[:, cc * L:(cc + 1) * L].reshape(nb, nk, L) + d_ref[c:c + 1, :] * uc
    inter = jnp.dot(h_prev, wc_ref[...], preferred_element_type=F32)
    for c in range(SSM_GROUP):
        yi_ref[c] = inter[:, c * L:(c + 1) * L].reshape(nb, nk, L)
    for b in range(nb):
        y_ref[b] = jnp.swapaxes(ys_ref[:, b] + yi_ref[:, b], 0, 1)


def _toeplitz_call(ut, h0, tabs):
    g, nb, nk, lanes = ut.shape
    L = TOEP_CHUNK
    rows = nb * nk
    lcr, lci, lrr, lri, cb, bb1, bb2, ct, dtab = tabs

    def per_group(*shape):
        return pl.BlockSpec((None,) + shape, lambda i: (i,) + (0,) * len(shape))

    return pl.pallas_call(
        _toeplitz_kernel,
        grid=(g,),
        in_specs=[
            per_group(nb, nk, lanes), per_group(nb, L),
            per_group(SSM_STATE, L), per_group(SSM_STATE, L), per_group(1, L), per_group(1, L),
            per_group(SSM_GROUP * SSM_GROUP, L), per_group(SSM_GROUP, L), per_group(SSM_GROUP, L),
            per_group(L, 2 * SSM_GROUP), per_group(SSM_GROUP, L),
        ],
        out_specs=[per_group(nb, nk, SSM_GROUP, L), per_group(nb, L)],
        out_shape=[
            jax.ShapeDtypeStruct((g, nb, nk, SSM_GROUP, L), F32),
            jax.ShapeDtypeStruct((g, nb, L), F32),
        ],
        scratch_shapes=[
            pltpu.VMEM((rows, lanes), BF16),
            pltpu.VMEM((lanes, L), BF16),
            pltpu.VMEM((L, lanes), BF16),
            pltpu.VMEM((SSM_GROUP // 2, lanes, 2 * L), BF16),
            pltpu.VMEM((SSM_GROUP, nb, nk, L), F32),
            pltpu.VMEM((SSM_GROUP, nb, nk, L), F32),
        ],
        compiler_params=pltpu.CompilerParams(
            dimension_semantics=("arbitrary",), vmem_limit_bytes=VMEM_LIMIT_BYTES),
        name="ssm_toeplitz",
    )(ut, h0, lcr, lci, lrr, lri, cb, bb1, bb2, ct, dtab)


def _layer_kernel(chunk, sub, y_by_group, x_ref, y_ref, q_ref, k_ref, kdk_ref, v_ref, s0_ref, ng_ref,
                  wg_ref, rg_ref, dm_ref, qd_ref, sd_ref, wglu_ref, bglu_ref, wo_ref, fg_ref,
                  out_ref, sfin_ref, o_ref):
    nb, tb, d = x_ref.shape
    rows = nb * tb

    @pl.when(pl.program_id(1) == 0)
    def _():
        sfin_ref[...] = s0_ref[...]

    def load_rows(ref, r0, lanes=slice(None)):
        if nb == 1:
            return ref[0, r0:r0 + sub, lanes]
        val = ref[r0 // tb:(r0 + sub) // tb, :, lanes]
        return val.reshape(sub, val.shape[-1])

    def store_rows(ref, r0, val):
        if nb == 1:
            ref[0, r0:r0 + sub, :] = val
        else:
            ref[r0 // tb:(r0 + sub) // tb] = val.reshape(sub // tb, tb, ref.shape[-1])

    rg = rg_ref[...]
    for r0 in range(0, rows, sub):
        x = load_rows(x_ref, r0)
        hn = _rms(x, ng_ref[...]).astype(BF16)
        gates = jnp.dot(hn, wg_ref[...], preferred_element_type=F32)

        for h in range(RET_HEADS):
            lanes = slice(h * HEAD_DIM, (h + 1) * HEAD_DIM)
            q, kb, kdk, vb = (load_rows(ref, r0, lanes) for ref in (q_ref, k_ref, kdk_ref, v_ref))
            dm = dm_ref[h]
            qd = qd_ref[h]
            sd = sd_ref[h]
            for c0 in range(0, sub, chunk):
                n = (r0 + c0) // tb
                r = slice(c0, c0 + chunk)
                qc, kc, vc = q[r], kb[r], vb[r]
                s = sfin_ref[n, h]
                scores = lax.dot_general(qc, kc, (((1,), (1,)), ((), ())), preferred_element_type=F32)
                p = (scores * dm).astype(BF16)
                o = jnp.dot(p, vc, preferred_element_type=F32)
                o = o + jnp.dot(qc, s.astype(BF16), preferred_element_type=F32) * qd
                sfin_ref[n, h] = sd * s + jnp.dot(kdk[r].T, vc, preferred_element_type=F32)
                o_ref[r0 + c0:r0 + c0 + chunk, lanes] = o

        g_ret = gates[:, :RET_WIDTH]
        g_ssm = gates[:, RET_WIDTH:]
        o_parts = []
        for h in range(RET_HEADS):
            lanes = slice(h * HEAD_DIM, (h + 1) * HEAD_DIM)
            o_parts.append(_rms(o_ref[r0:r0 + sub, lanes], rg[:, lanes]))
        o = jnp.concatenate(o_parts, axis=-1) * jax.nn.silu(g_ret)

        if y_by_group:
            k0 = r0 // TOEP_CHUNK
            y = jnp.concatenate(
                [y_ref[:, k0 + j].reshape(SSM_WIDTH, TOEP_CHUNK).T for j in range(sub // TOEP_CHUNK)], axis=0)
        else:
            y = load_rows(y_ref, r0)
        z = jax.nn.gelu(y)
        gate = jnp.dot(z.astype(BF16), wglu_ref[...], preferred_element_type=F32) + bglu_ref[...]
        z = z * jax.nn.sigmoid(gate)
        z = z * jax.nn.silu(g_ssm)

        mix = jnp.concatenate([o, z], axis=-1).astype(BF16)
        res = x + jnp.dot(mix, wo_ref[...], preferred_element_type=F32)
        store_rows(out_ref, r0, _rms(res, fg_ref[...]))


def _layer_call(x, y, q, k, kdk, v, s0, norm_g, w_gates, ret_norm_g, dm, qd, sd, w_glu, b_glu,
                w_out, final_g, plan):
    b, t, d = x.shape
    nb, tb, chunk, sub = plan.nb, plan.tb, plan.chunk, plan.sub
    rows = nb * tb
    c2 = lambda i, j: (0, 0)
    c3 = lambda i, j: (0, 0, 0)
    state_spec = pl.BlockSpec((nb, RET_HEADS, HEAD_DIM, HEAD_DIM), lambda i, j: (i, 0, 0, 0))
    row_spec = pl.BlockSpec((nb, tb, RET_WIDTH), lambda i, j: (i, j, 0))
    if plan.toeplitz:
        y_spec = pl.BlockSpec((SSM_GROUPS, None, tb // TOEP_CHUNK, SSM_GROUP, TOEP_CHUNK),
                              lambda i, j: (0, i, j, 0, 0))
    else:
        y_spec = pl.BlockSpec((nb, tb, SSM_WIDTH), lambda i, j: (i, j, 0))
    return pl.pallas_call(
        functools.partial(_layer_kernel, chunk, sub, plan.toeplitz),
        grid=(b // nb, t // tb),
        in_specs=[
            pl.BlockSpec((nb, tb, d), lambda i, j: (i, j, 0)),
            y_spec,
            row_spec, row_spec, row_spec, row_spec,
            state_spec,
            pl.BlockSpec((1, d), c2),
            pl.BlockSpec((d, RET_WIDTH + SSM_WIDTH), c2),
            pl.BlockSpec((1, RET_WIDTH), c2),
            pl.BlockSpec((RET_HEADS, chunk, chunk), c3),
            pl.BlockSpec((RET_HEADS, chunk, HEAD_DIM), c3),
            pl.BlockSpec((RET_HEADS, HEAD_DIM, HEAD_DIM), c3),
            pl.BlockSpec((SSM_WIDTH, SSM_WIDTH), c2),
            pl.BlockSpec((1, SSM_WIDTH), c2),
            pl.BlockSpec((D_MODEL, D_MODEL), c2),
            pl.BlockSpec((1, d), c2),
        ],
        out_specs=[
            pl.BlockSpec((nb, tb, d), lambda i, j: (i, j, 0)),
            state_spec,
        ],
        out_shape=[
            jax.ShapeDtypeStruct((b, t, d), F32),
            jax.ShapeDtypeStruct((b, RET_HEADS, HEAD_DIM, HEAD_DIM), F32),
        ],
        scratch_shapes=[pltpu.VMEM((rows, RET_WIDTH), F32)],
        compiler_params=pltpu.CompilerParams(
            dimension_semantics=("arbitrary", "arbitrary"), vmem_limit_bytes=VMEM_LIMIT_BYTES),
        name="retention_glu_layer",
    )(x, y, q, k, kdk, v, s0, norm_g, w_gates, ret_norm_g, dm, qd, sd, w_glu, b_glu, w_out, final_g)


def _rotary_tables(pos0, steps, reps):
    f = np.float64
    half = HEAD_DIM // 2
    inv = f(ROPE_BASE) ** (-np.arange(half, dtype=f) / f(half))
    ang = (f(pos0) + np.arange(steps, dtype=f))[:, None] * inv[None, :]
    cos, sin = np.cos(ang), np.sin(ang)
    cos2 = np.concatenate([cos, cos], axis=-1)
    sin2 = np.concatenate([-sin, sin], axis=-1)
    return jnp.asarray(np.tile(cos2, (reps, 1)), F32), jnp.asarray(np.tile(sin2, (reps, 1)), F32)


def _retention_tables(chunk, sub):
    f = np.float64
    log_g = np.log1p(-np.power(f(2.0), f(-5.0) - np.arange(RET_HEADS, dtype=f)))
    idx = np.arange(chunk, dtype=f)
    diff = idx[:, None] - idx[None, :]
    causal = diff >= 0
    dm = np.where(causal[None], np.exp(np.where(causal, diff, f(0.0))[None] * log_g[:, None, None]), f(0.0))
    q_decay = np.exp((idx + f(1.0))[None, :] * log_g[:, None])
    k_decay = np.exp((f(chunk) - f(1.0) - idx)[None, :] * log_g[:, None])
    s_decay = np.exp(f(chunk) * log_g)
    qd = np.broadcast_to(q_decay[:, :, None], (RET_HEADS, chunk, HEAD_DIM))
    sd = np.broadcast_to(s_decay[:, None, None], (RET_HEADS, HEAD_DIM, HEAD_DIM))
    kd = np.broadcast_to(k_decay.T[:, :, None], (chunk, RET_HEADS, HEAD_DIM)).reshape(chunk, RET_WIDTH)
    kd_rows = np.tile(kd, (sub // chunk, 1))
    return tuple(jnp.asarray(np.ascontiguousarray(a), F32) for a in (dm, qd, sd, kd_rows))


def _discretise(lambda_re, lambda_im, log_step, b_re, b_im):
    a = jnp.minimum(lambda_re.astype(F32), LAMBDA_RE_MAX)
    b = lambda_im.astype(F32)
    dt = jnp.exp(log_step.astype(F32))[:, None]
    mag = jnp.exp(a * dt)
    lre, lim = mag * jnp.cos(b * dt), mag * jnp.sin(b * dt)
    den = a * a + b * b
    cr = ((lre - 1.0) * a + lim * b) / den
    ci = (lim * a - (lre - 1.0) * b) / den
    b_re, b_im = b_re.astype(F32), b_im.astype(F32)
    bre = cr[..., None] * b_re - ci[..., None] * b_im
    bim = cr[..., None] * b_im + ci[..., None] * b_re
    return (lre, lim), (bre, bim)


def _scan_tables(lam_bar, b_bar, c_re, c_im):
    gl = SSM_GROUPS // SSM_HALVES
    eye = jnp.eye(gl, dtype=F32)
    ar = lam_bar[0].reshape(SSM_HALVES, HALF_RI)
    ai = lam_bar[1].reshape(SSM_HALVES, HALF_RI)

    def place_b(m):
        m = m.reshape(SSM_HALVES, gl, SSM_STATE, SSM_GROUP)
        return jnp.einsum("hgpc,gk->hgckp", m, eye).reshape(SSM_HALVES, HALF_IN, HALF_RI)

    def place_c(m):
        m = m.reshape(SSM_HALVES, gl, SSM_GROUP, SSM_STATE)
        return jnp.einsum("hgcp,gk->hkpgc", m, eye).reshape(SSM_HALVES, HALF_RI, HALF_IN)

    bd = jnp.concatenate([place_b(b_bar[0]), place_b(b_bar[1])], axis=-1).astype(BF16)
    cd = jnp.concatenate([place_c(c_re.astype(F32)), place_c(-c_im.astype(F32))], axis=1).astype(BF16)
    return ar, ai, bd, cd


def _toeplitz_tables(lam_bar, b_bar, c_re, c_im, d):
    g, p = lam_bar[0].shape
    L = TOEP_CHUNK
    lre, lim = lam_bar
    bre, bim = b_bar
    cre, cim = c_re.astype(F32), c_im.astype(F32)
    lcr = jnp.broadcast_to(lre[:, :, None], (g, p, L))
    lci = jnp.broadcast_to(lim[:, :, None], (g, p, L))
    lrr = jnp.concatenate([lre, lre], axis=-1)[:, None, :]
    lri = jnp.concatenate([lim, lim], axis=-1)[:, None, :]
    cb_re = cre[:, None, :, :] * bre.transpose(0, 2, 1)[:, :, None, :] - cim[:, None, :, :] * bim.transpose(0, 2, 1)[:, :, None, :]
    cb_im = cre[:, None, :, :] * bim.transpose(0, 2, 1)[:, :, None, :] + cim[:, None, :, :] * bre.transpose(0, 2, 1)[:, :, None, :]
    cb = jnp.concatenate([cb_re, cb_im], axis=-1).reshape(g, SSM_GROUP * SSM_GROUP, 2 * p)
    bt_re, bt_im = bre.transpose(0, 2, 1), bim.transpose(0, 2, 1)
    bb1 = jnp.concatenate([bt_re, bt_im], axis=-1)
    bb2 = jnp.concatenate([-bt_im, bt_re], axis=-1)
    ct_re, ct_im = cre.transpose(0, 2, 1), cim.transpose(0, 2, 1)
    col_a = jnp.concatenate([ct_re, -ct_re], axis=1)
    col_b = jnp.concatenate([-ct_im, -ct_im], axis=1)
    ct = jnp.concatenate([col_a, col_b], axis=-1)
    dtab = jnp.broadcast_to(d.astype(F32).reshape(g, SSM_GROUP, 1), (g, SSM_GROUP, L))
    return lcr, lci, lrr, lri, cb, bb1, bb2, ct, dtab


def _state_to_lanes(re, im):
    b = re.shape[0]
    re = re.astype(F32).reshape(b, SSM_HALVES, HALF_RI)
    im = im.astype(F32).reshape(b, SSM_HALVES, HALF_RI)
    return jnp.concatenate([re, im], axis=-1).reshape(b, N_STATE)


def _lanes_to_state(h):
    b = h.shape[0]
    h = h.reshape(b, SSM_HALVES, 2, HALF_RI)
    return (h[:, :, 0].reshape(b, SSM_GROUPS, SSM_STATE), h[:, :, 1].reshape(b, SSM_GROUPS, SSM_STATE))


class _Plan(NamedTuple):
    nb: int
    tb: int
    chunk: int
    sub: int
    toeplitz: bool


def _plan(b, t):
    if t >= LAYER_ROWS:
        assert t % LAYER_ROWS == 0 and LAYER_ROWS % LAYER_CHUNK == 0 and LAYER_CHUNK % TOEP_CHUNK == 0
        return _Plan(nb=1, tb=LAYER_ROWS, chunk=LAYER_CHUNK, sub=LAYER_CHUNK, toeplitz=True)
    nb = min(b, LAYER_ROWS // t)
    assert b % nb == 0
    return _Plan(nb=nb, tb=t, chunk=t, sub=nb * t, toeplitz=False)


def _stream_step(x, pos0, ret_s0, h0_re, h0_im, weights):
    (norm_g, w_qkv, w_gates, w_u, ret_norm_g, lam_bar, b_bar, c_re, c_im, ssm_d, w_glu, b_glu, w_out,
     final_g) = weights
    b, t, _ = x.shape
    plan = _plan(b, t)
    cos, sin = _rotary_tables(pos0, t, plan.nb)
    dm, qd, sd, kd_rows = _retention_tables(plan.chunk, plan.sub)
    front = _front_call(x, norm_g, w_u.T if plan.toeplitz else None, w_qkv, cos, sin, kd_rows,
                        plan.nb, plan.tb, plan.sub, plan.toeplitz)
    if plan.toeplitz:
        ut, q, k, kdk, v = front
        h0 = jnp.concatenate([h0_re, h0_im], axis=-1).astype(F32).transpose(1, 0, 2)
        y, h_fin = _toeplitz_call(ut, h0, _toeplitz_tables(lam_bar, b_bar, c_re, c_im, ssm_d))
        h_fin = h_fin.transpose(1, 0, 2)
        h_re, h_im = h_fin[..., :SSM_STATE], h_fin[..., SSM_STATE:]
    else:
        q, k, kdk, v = front
        ar, ai, bd, cd = _scan_tables(lam_bar, b_bar, c_re, c_im)
        y, h_fin = _ssm_call(x, norm_g, w_u, _state_to_lanes(h0_re, h0_im), ar, ai, bd, cd,
                             ssm_d[None, :].astype(F32), plan.tb)
        h_re, h_im = _lanes_to_state(h_fin)
    out, s_fin = _layer_call(x, y, q, k, kdk, v, ret_s0, norm_g, w_gates, ret_norm_g, dm, qd, sd,
                             w_glu, b_glu, w_out, final_g, plan)
    return out, s_fin, h_re, h_im


def kernel(x_prompt, x_sample, state_ret, state_ssm_re, state_ssm_im, norm_g, w_in, ret_norm_g,
           ssm_lambda_re, ssm_lambda_im, ssm_log_step, ssm_b_re, ssm_b_im, ssm_c_re, ssm_c_im,
           ssm_d, w_glu, b_glu, w_out, final_norm_g):
    depth = w_in.shape[0]
    assert depth == 1, "single-layer trunk"
    bp, tp, _ = x_prompt.shape
    bs, ts, _ = x_sample.shape
    l = 0
    w = w_in[l]
    u_lo = 4 * RET_WIDTH
    w_qkv = w[:, :3 * RET_WIDTH].astype(BF16)
    w_gates = jnp.concatenate([w[:, 3 * RET_WIDTH:u_lo], w[:, u_lo + SSM_WIDTH:]], axis=1).astype(BF16)
    w_u = w[:, u_lo:u_lo + SSM_WIDTH].astype(BF16)
    lam_bar, b_bar = _discretise(ssm_lambda_re[l], ssm_lambda_im[l], ssm_log_step[l], ssm_b_re[l], ssm_b_im[l])
    weights = (
        norm_g[l][None, :].astype(F32), w_qkv, w_gates, w_u, ret_norm_g[l].reshape(1, RET_WIDTH).astype(F32),
        lam_bar, b_bar, ssm_c_re[l], ssm_c_im[l], ssm_d[l],
        w_glu[l].astype(BF16), b_glu[l][None, :].astype(F32),
        w_out[l].astype(BF16), final_norm_g[None, :].astype(F32),
    )

    ret0_p = jnp.zeros((bp, RET_HEADS, HEAD_DIM, HEAD_DIM), F32)
    h0_p = jnp.zeros((bp, SSM_GROUPS, SSM_STATE), F32)
    y_p, r_p, hp_re, hp_im = _stream_step(x_prompt, 0, ret0_p, h0_p, h0_p, weights)
    y_s, r_s, hs_re, hs_im = _stream_step(x_sample, PAST_LEN, state_ret[l].astype(F32),
                                          state_ssm_re[l], state_ssm_im[l], weights)
    return (y_p, y_s, r_p[None], hp_re[None], hp_im[None], r_s[None], hs_re[None], hs_im[None])
```

```python
import functools
from typing import NamedTuple

import jax
import jax.numpy as jnp
import numpy as np
from jax import lax
from jax.experimental import pallas as pl
from jax.experimental.pallas import tpu as pltpu

D_MODEL = 1024
RET_HEADS = 4
HEAD_DIM = 128
RET_WIDTH = RET_HEADS * HEAD_DIM
SSM_WIDTH = 512
SSM_GROUP = 16
SSM_GROUPS = 32
SSM_STATE = 64
ROPE_BASE = 10000.0
EPS = 1e-6
LAMBDA_RE_MAX = -1e-4
PAST_LEN = 2048

LAYER_ROWS = 1024
LAYER_CHUNK = 256
TOEP_CHUNK = 128
TOEP_LANES = SSM_GROUP * TOEP_CHUNK

W_IN_G_RET_BLOCK = 3
W_IN_U_BLOCK = 4
W_IN_G_SSM_BLOCK = 5

SSM_HALVES = 2
HALF_IN = SSM_WIDTH // SSM_HALVES
HALF_RI = (SSM_GROUPS // SSM_HALVES) * SSM_STATE
HALF_STATE = 2 * HALF_RI
N_STATE = SSM_HALVES * HALF_STATE
SCAN_LANES = 512

VMEM_LIMIT_BYTES = 56 * 1024 * 1024

F32 = jnp.float32
BF16 = jnp.bfloat16


def _rms(x, g):
    return x * lax.rsqrt(jnp.mean(x * x, axis=-1, keepdims=True) + EPS) * g


def _ssm_kernel(x_ref, g_ref, wu_ref, h0_ref, ar_ref, ai_ref, bd_ref, cd_ref, d_ref, y_ref, hfin_ref,
                ubt_ref, utb_ref, bu_ref, hs_ref, ytb_ref):
    nb, tb, d_model = x_ref.shape
    w = SSM_WIDTH
    rows = nb * tb

    @pl.when(pl.program_id(0) == 0)
    def _():
        hfin_ref[...] = h0_ref[...]

    hn = _rms(x_ref[...].reshape(rows, d_model), g_ref[...]).astype(BF16)
    ubt_ref[...] = jnp.dot(hn, wu_ref[...], preferred_element_type=F32).reshape(nb, tb, w)
    for t in range(tb):
        utb_ref[t] = ubt_ref[:, t, :]
    u = utb_ref[...].reshape(rows, w)
    ub = u.astype(BF16)

    for h in range(SSM_HALVES):
        cols = slice(h * HALF_IN, (h + 1) * HALF_IN)
        yh = d_ref[:, cols] * u[:, cols]
        for q in range(HALF_RI // SCAN_LANES):
            re_h = q * SCAN_LANES
            im_h = HALF_RI + re_h
            re = h * HALF_STATE + re_h
            im = h * HALF_STATE + im_h
            bu_ref[:, re:re + SCAN_LANES] = jnp.dot(
                ub[:, cols], bd_ref[h, :, re_h:re_h + SCAN_LANES], preferred_element_type=F32)
            bu_ref[:, im:im + SCAN_LANES] = jnp.dot(
                ub[:, cols], bd_ref[h, :, im_h:im_h + SCAN_LANES], preferred_element_type=F32)
            ar = jnp.broadcast_to(ar_ref[h:h + 1, re_h:re_h + SCAN_LANES], (nb, SCAN_LANES))
            ai = jnp.broadcast_to(ai_ref[h:h + 1, re_h:re_h + SCAN_LANES], (nb, SCAN_LANES))
            hr = hfin_ref[:, re:re + SCAN_LANES]
            hi = hfin_ref[:, im:im + SCAN_LANES]
            for t in range(tb):
                r = slice(t * nb, (t + 1) * nb)
                hr, hi = (ar * hr - ai * hi + bu_ref[r, re:re + SCAN_LANES],
                          ar * hi + ai * hr + bu_ref[r, im:im + SCAN_LANES])
                hs_ref[r, re:re + SCAN_LANES] = hr.astype(BF16)
                hs_ref[r, im:im + SCAN_LANES] = hi.astype(BF16)
            hfin_ref[:, re:re + SCAN_LANES] = hr
            hfin_ref[:, im:im + SCAN_LANES] = hi
            yh = yh + jnp.dot(hs_ref[:, re:re + SCAN_LANES], cd_ref[h, re_h:re_h + SCAN_LANES, :],
                              preferred_element_type=F32)
            yh = yh + jnp.dot(hs_ref[:, im:im + SCAN_LANES], cd_ref[h, im_h:im_h + SCAN_LANES, :],
                              preferred_element_type=F32)
        ytb_ref[:, :, cols] = yh.reshape(tb, nb, HALF_IN)
    for b in range(nb):
        y_ref[b] = ytb_ref[:, b, :]


def _ssm_call(x, norm_g, w_u, h0, ar, ai, bd, cd, d, tb):
    nb, t, d_model = x.shape
    w = SSM_WIDTH
    rows = nb * tb
    n_blocks = t // tb
    c2 = lambda i: (0, 0)
    c3 = lambda i: (0, 0, 0)
    return pl.pallas_call(
        _ssm_kernel,
        grid=(n_blocks,),
        in_specs=[
            pl.BlockSpec((nb, tb, d_model), lambda i: (0, i, 0)),
            pl.BlockSpec((1, d_model), c2),
            pl.BlockSpec((d_model, w), lambda i: (0, W_IN_U_BLOCK)),
            pl.BlockSpec((nb, N_STATE), c2),
            pl.BlockSpec((SSM_HALVES, HALF_RI), c2),
            pl.BlockSpec((SSM_HALVES, HALF_RI), c2),
            pl.BlockSpec((SSM_HALVES, HALF_IN, HALF_STATE), c3),
            pl.BlockSpec((SSM_HALVES, HALF_STATE, HALF_IN), c3),
            pl.BlockSpec((1, w), c2),
        ],
        out_specs=[
            pl.BlockSpec((nb, tb, w), lambda i: (0, i, 0)),
            pl.BlockSpec((nb, N_STATE), c2),
        ],
        out_shape=[
            jax.ShapeDtypeStruct((nb, t, w), F32),
            jax.ShapeDtypeStruct((nb, N_STATE), F32),
        ],
        scratch_shapes=[
            pltpu.VMEM((nb, tb, w), F32),
            pltpu.VMEM((tb, nb, w), F32),
            pltpu.VMEM((rows, N_STATE), F32),
            pltpu.VMEM((rows, N_STATE), BF16),
            pltpu.VMEM((tb, nb, w), F32),
        ],
        compiler_params=pltpu.CompilerParams(
            dimension_semantics=("arbitrary",), vmem_limit_bytes=VMEM_LIMIT_BYTES),
        name="ssm_scan",
    )(x, norm_g, w_u, h0, ar, ai, bd, cd, d)


def _front_kernel(emit_ut, sub, x_ref, g_ref, *refs):
    if emit_ut:
        wut_ref, wqkv_ref, cos_ref, sin_ref, kd_ref, ut_ref, q_ref, k_ref, kdk_ref, v_ref = refs
    else:
        wqkv_ref, cos_ref, sin_ref, kd_ref, q_ref, k_ref, kdk_ref, v_ref = refs
    nb, tb, d = x_ref.shape
    rows = nb * tb

    def load_rows(ref, r0):
        if nb == 1:
            return ref[0, r0:r0 + sub, :]
        return ref[r0 // tb:(r0 + sub) // tb].reshape(sub, ref.shape[-1])

    def store_rows(ref, r0, lanes, val):
        if nb == 1:
            ref[0, r0:r0 + sub, lanes] = val
        else:
            ref[r0 // tb:(r0 + sub) // tb, :, lanes] = val.reshape(sub // tb, tb, val.shape[-1])

    hns = []
    for r0 in range(0, rows, sub):
        hn = _rms(load_rows(x_ref, r0), g_ref[...]).astype(BF16)
        hns.append(hn)
        qkv = jnp.dot(hn, wqkv_ref[...], preferred_element_type=F32)
        cos = cos_ref[r0:r0 + sub, :]
        sin = sin_ref[r0:r0 + sub, :]

        def rotary(a):
            return a * cos + pltpu.roll(a, HEAD_DIM // 2, 1) * sin

        for h in range(RET_HEADS):
            lanes = slice(h * HEAD_DIM, (h + 1) * HEAD_DIM)
            k = rotary(qkv[:, RET_WIDTH + h * HEAD_DIM:RET_WIDTH + (h + 1) * HEAD_DIM]) * (HEAD_DIM ** -0.5)
            store_rows(q_ref, r0, lanes, rotary(qkv[:, lanes]).astype(BF16))
            store_rows(k_ref, r0, lanes, k.astype(BF16))
            store_rows(kdk_ref, r0, lanes, (k * kd_ref[:, lanes]).astype(BF16))
            store_rows(v_ref, r0, lanes,
                       qkv[:, 2 * RET_WIDTH + h * HEAD_DIM:2 * RET_WIDTH + (h + 1) * HEAD_DIM].astype(BF16))

    if emit_ut:
        nk = rows // TOEP_CHUNK
        hn = jnp.concatenate(hns, axis=0)
        ut = lax.dot_general(wut_ref[...], hn, (((1,), (1,)), ((), ())), preferred_element_type=F32)
        for g in range(SSM_GROUPS):
            grp = slice(g * SSM_GROUP, (g + 1) * SSM_GROUP)
            slabs = jnp.stack([ut[grp, kk * TOEP_CHUNK:(kk + 1) * TOEP_CHUNK] for kk in range(nk)])
            by_channel = jnp.swapaxes(slabs, 0, 1)
            for c in range(SSM_GROUP):
                ut_ref[g, :, c * TOEP_CHUNK:(c + 1) * TOEP_CHUNK] = by_channel[c]


def _front_call(x, norm_g, w_ut, w_qkv, cos, sin, kd_rows, nb, tb, sub, emit_ut):
    b, t, d = x.shape
    rows = nb * tb
    c2 = lambda i, j: (0, 0)
    row_spec = pl.BlockSpec((nb, tb, RET_WIDTH), lambda i, j: (i, j, 0))
    in_specs = [pl.BlockSpec((nb, tb, d), lambda i, j: (i, j, 0)), pl.BlockSpec((1, d), c2)]
    operands = [x, norm_g]
    out_specs, out_shape = [], []
    if emit_ut:
        in_specs.append(pl.BlockSpec((SSM_WIDTH, d), c2))
        operands.append(w_ut)
        out_specs.append(pl.BlockSpec((SSM_GROUPS, None, tb // TOEP_CHUNK, TOEP_LANES), lambda i, j: (0, i, j, 0)))
        out_shape.append(jax.ShapeDtypeStruct((SSM_GROUPS, b, t // TOEP_CHUNK, TOEP_LANES), F32))
    in_specs += [
        pl.BlockSpec((d, 3 * RET_WIDTH), c2),
        pl.BlockSpec((rows, HEAD_DIM), lambda i, j: (j, 0)),
        pl.BlockSpec((rows, HEAD_DIM), lambda i, j: (j, 0)),
        pl.BlockSpec((sub, RET_WIDTH), c2),
    ]
    operands += [w_qkv, cos, sin, kd_rows]
    out_specs += [row_spec] * 4
    out_shape += [jax.ShapeDtypeStruct((b, t, RET_WIDTH), BF16)] * 4
    return pl.pallas_call(
        functools.partial(_front_kernel, emit_ut, sub),
        grid=(b // nb, t // tb),
        in_specs=in_specs,
        out_specs=out_specs,
        out_shape=out_shape,
        compiler_params=pltpu.CompilerParams(
            dimension_semantics=("arbitrary", "arbitrary"), vmem_limit_bytes=VMEM_LIMIT_BYTES),
        name="qkv_rotary_front",
    )(*operands)


def _powers(base_re, base_im, exponent, n_bits):
    pr = jnp.ones_like(base_re)
    pi = jnp.zeros_like(base_re)
    br, bi = base_re, base_im
    for bit in range(n_bits):
        take = ((exponent >> bit) & 1) == 1
        pr, pi = jnp.where(take, pr * br - pi * bi, pr), jnp.where(take, pr * bi + pi * br, pi)
        br, bi = br * br - bi * bi, 2.0 * br * bi
    return pr, pi


def _split_bf16(x):
    hi = x.astype(BF16)
    r = x - hi.astype(F32)
    mid = r.astype(BF16)
    lo = (r - mid.astype(F32)).astype(BF16)
    return hi, mid, lo


def _dot_f32(a, b):
    a3, b3 = _split_bf16(a), _split_bf16(b)
    acc = None
    for i in range(3):
        for j in range(3 - i):
            term = jnp.dot(a3[i], b3[j], preferred_element_type=F32)
            acc = term if acc is None else acc + term
    return acc


def _toeplitz_kernel(u_ref, h0_ref, lcr_ref, lci_ref, lrr_ref, lri_ref, cb_ref, bb1_ref, bb2_ref,
                     ct_ref, d_ref, y_ref, hfin_ref, a_ref, wb_ref, wc_ref, t_ref, ys_ref, yi_ref):
    nb, nk, _ = u_ref.shape
    rows = nb * nk
    L = TOEP_CHUNK
    n_bits = L.bit_length() - 1
    sub_i = lax.broadcasted_iota(jnp.int32, (L, L), 0)
    lane_i = lax.broadcasted_iota(jnp.int32, (L, L), 1)

    lcr, lci = lcr_ref[...], lci_ref[...]
    lane_p = lax.broadcasted_iota(jnp.int32, lcr.shape, 1)
    p0r, p0i = _powers(lcr, lci, lane_p, n_bits)
    p1r, p1i = p0r * lcr - p0i * lci, p0r * lci + p0i * lcr

    taps = _dot_f32(cb_ref[...], jnp.concatenate([p0r, -p0i], axis=0))

    lrr = jnp.broadcast_to(lrr_ref[...], (L, L))
    lri = jnp.broadcast_to(lri_ref[...], (L, L))
    qr, qi = _powers(lrr, lri, (L - 1) - sub_i, n_bits)
    for c in range(SSM_GROUP):
        wb_ref[c * L:(c + 1) * L, :] = (qr * bb1_ref[c:c + 1, :] + qi * bb2_ref[c:c + 1, :]).astype(BF16)
    pa = jnp.concatenate([p1r, p1i], axis=0)
    pb = jnp.concatenate([p1i, p1r], axis=0)
    for c in range(SSM_GROUP):
        wc_ref[:, c * L:(c + 1) * L] = (
            jnp.broadcast_to(ct_ref[:, c:c + 1], (L, L)) * pa
            + jnp.broadcast_to(ct_ref[:, SSM_GROUP + c:SSM_GROUP + c + 1], (L, L)) * pb).astype(BF16)

    a_ref[...] = u_ref[...].reshape(rows, TOEP_LANES).astype(BF16)

    s_in = jnp.dot(a_ref[...], wb_ref[...], preferred_element_type=F32)
    s_in = jnp.swapaxes(s_in.reshape(nb, nk, L), 0, 1)
    l128r = qr[0:1] * lrr[0:1] - qi[0:1] * lri[0:1]
    l128i = qr[0:1] * lri[0:1] + qi[0:1] * lrr[0:1]
    half = L // 2
    is_re = lane_i[0:1] < half
    mul_same = jnp.broadcast_to(l128r, (nb, L))
    mul_swap = jnp.broadcast_to(jnp.where(is_re, -l128i, l128i), (nb, L))
    s_sw = pltpu.roll(s_in.reshape(nk * nb, L), half, 1).reshape(nk, nb, L)
    h = h0_ref[...]
    h_sw = pltpu.roll(h, half, 1)
    h_prev = []
    for k in range(nk):
        h_prev.append(h)
        h, h_sw = (h * mul_same + h_sw * mul_swap + s_in[k],
                   h_sw * mul_same - h * mul_swap + s_sw[k])
    hfin_ref[...] = h
    h_prev = jnp.swapaxes(jnp.stack(h_prev), 0, 1).reshape(rows, L).astype(BF16)

    causal = lane_i >= sub_i
    for pair in range(SSM_GROUP // 2):
        slot = pair
        for ci in range(SSM_GROUP):
            for cc in range(2):
                r = ci * SSM_GROUP + 2 * pair + cc
                tap = jnp.broadcast_to(taps[r:r + 1, :], (L, L))
                toep = jnp.where(causal, pltpu.roll(tap, 0, 1, stride=1, stride_axis=0), 0.0)
                t_ref[slot, ci * L:(ci + 1) * L, cc * L:(cc + 1) * L] = toep.astype(BF16)
        yv = jnp.dot(a_ref[...], t_ref[slot], preferred_element_type=F32)
        for cc in range(2):
            c = 2 * pair + cc
            uc = u_ref[:, :, c * L:(c + 1) * L]
            ys_ref[c] = yv[:, cc * L:(cc + 1) * L].reshape(nb, nk, L) + d_ref[c:c + 1, :] * uc
    inter = jnp.dot(h_prev, wc_ref[...], preferred_element_type=F32)
    for c in range(SSM_GROUP):
        yi_ref[c] = inter[:, c * L:(c + 1) * L].reshape(nb, nk, L)
    for b in range(nb):
        y_ref[b] = jnp.swapaxes(ys_ref[:, b] + yi_ref[:, b], 0, 1)


def _toeplitz_call(ut, h0, tabs):
    g, nb, nk, lanes = ut.shape
    L = TOEP_CHUNK
    rows = nb * nk
    lcr, lci, lrr, lri, cb, bb1, bb2, ct, dtab = tabs

    def per_group(*shape):
        return pl.BlockSpec((None,) + shape, lambda i: (i,) + (0,) * len(shape))

    return pl.pallas_call(
        _toeplitz_kernel,
        grid=(g,),
        in_specs=[
            per_group(nb, nk, lanes), per_group(nb, L),
            per_group(SSM_STATE, L), per_group(SSM_STATE, L), per_group(1, L), per_group(1, L),
            per_group(SSM_GROUP * SSM_GROUP, L), per_group(SSM_GROUP, L), per_group(SSM_GROUP, L),
            per_group(L, 2 * SSM_GROUP), per_group(SSM_GROUP, L),
        ],
        out_specs=[per_group(nb, nk, SSM_GROUP, L), per_group(nb, L)],
        out_shape=[
            jax.ShapeDtypeStruct((g, nb, nk, SSM_GROUP, L), F32),
            jax.ShapeDtypeStruct((g, nb, L), F32),
        ],
        scratch_shapes=[
            pltpu.VMEM((rows, lanes), BF16),
            pltpu.VMEM((lanes, L), BF16),
            pltpu.VMEM((L, lanes), BF16),
            pltpu.VMEM((SSM_GROUP // 2, lanes, 2 * L), BF16),
            pltpu.VMEM((SSM_GROUP, nb, nk, L), F32),
            pltpu.VMEM((SSM_GROUP, nb, nk, L), F32),
        ],
        compiler_params=pltpu.CompilerParams(
            dimension_semantics=("arbitrary",), vmem_limit_bytes=VMEM_LIMIT_BYTES),
        name="ssm_toeplitz",
    )(ut, h0, lcr, lci, lrr, lri, cb, bb1, bb2, ct, dtab)


def _layer_kernel(chunk, sub, y_by_group, x_ref, y_ref, q_ref, k_ref, kdk_ref, v_ref, s0_ref, ng_ref,
                  wgr_ref, wgs_ref, rg_ref, dm_ref, qd_ref, sd_ref, wglu_ref, bglu_ref, wo_ref, fg_ref,
                  out_ref, sfin_ref, o_ref):
    nb, tb, d = x_ref.shape
    rows = nb * tb

    @pl.when(pl.program_id(1) == 0)
    def _():
        sfin_ref[...] = s0_ref[...]

    def load_rows(ref, r0, lanes=slice(None)):
        if nb == 1:
            return ref[0, r0:r0 + sub, lanes]
        val = ref[r0 // tb:(r0 + sub) // tb, :, lanes]
        return val.reshape(sub, val.shape[-1])

    def store_rows(ref, r0, val):
        if nb == 1:
            ref[0, r0:r0 + sub, :] = val
        else:
            ref[r0 // tb:(r0 + sub) // tb] = val.reshape(sub // tb, tb, ref.shape[-1])

    rg = rg_ref[...]
    for r0 in range(0, rows, sub):
        x = load_rows(x_ref, r0)
        hn = _rms(x, ng_ref[...]).astype(BF16)
        g_ret = jnp.dot(hn, wgr_ref[...], preferred_element_type=F32)
        g_ssm = jnp.dot(hn, wgs_ref[...], preferred_element_type=F32)

        for h in range(RET_HEADS):
            lanes = slice(h * HEAD_DIM, (h + 1) * HEAD_DIM)
            q, kb, kdk, vb = (load_rows(ref, r0, lanes) for ref in (q_ref, k_ref, kdk_ref, v_ref))
            dm = dm_ref[h]
            qd = qd_ref[h]
            sd = sd_ref[h]
            for c0 in range(0, sub, chunk):
                n = (r0 + c0) // tb
                r = slice(c0, c0 + chunk)
                qc, kc, vc = q[r], kb[r], vb[r]
                s = sfin_ref[n, h]
                scores = lax.dot_general(qc, kc, (((1,), (1,)), ((), ())), preferred_element_type=F32)
                p = (scores * dm).astype(BF16)
                o = jnp.dot(p, vc, preferred_element_type=F32)
                o = o + jnp.dot(qc, s.astype(BF16), preferred_element_type=F32) * qd
                sfin_ref[n, h] = sd * s + jnp.dot(kdk[r].T, vc, preferred_element_type=F32)
                o_ref[r0 + c0:r0 + c0 + chunk, lanes] = o

        o_parts = []
        for h in range(RET_HEADS):
            lanes = slice(h * HEAD_DIM, (h + 1) * HEAD_DIM)
            o_parts.append(_rms(o_ref[r0:r0 + sub, lanes], rg[:, lanes]))
        o = jnp.concatenate(o_parts, axis=-1) * jax.nn.silu(g_ret)

        if y_by_group:
            k0 = r0 // TOEP_CHUNK
            y = jnp.concatenate(
                [y_ref[:, k0 + j].reshape(SSM_WIDTH, TOEP_CHUNK).T for j in range(sub // TOEP_CHUNK)], axis=0)
        else:
            y = load_rows(y_ref, r0)
        z = jax.nn.gelu(y)
        gate = jnp.dot(z.astype(BF16), wglu_ref[...], preferred_element_type=F32) + bglu_ref[...]
        z = z * jax.nn.sigmoid(gate)
        z = z * jax.nn.silu(g_ssm)

        mix = jnp.concatenate([o, z], axis=-1).astype(BF16)
        res = x + jnp.dot(mix, wo_ref[...], preferred_element_type=F32)
        store_rows(out_ref, r0, _rms(res, fg_ref[...]))


def _layer_call(x, y, q, k, kdk, v, s0, norm_g, w_in, ret_norm_g, dm, qd, sd, w_glu, b_glu,
                w_out, final_g, plan):
    b, t, d = x.shape
    nb, tb, chunk, sub = plan.nb, plan.tb, plan.chunk, plan.sub
    rows = nb * tb
    c2 = lambda i, j: (0, 0)
    c3 = lambda i, j: (0, 0, 0)
    state_spec = pl.BlockSpec((nb, RET_HEADS, HEAD_DIM, HEAD_DIM), lambda i, j: (i, 0, 0, 0))
    row_spec = pl.BlockSpec((nb, tb, RET_WIDTH), lambda i, j: (i, j, 0))
    if plan.toeplitz:
        y_spec = pl.BlockSpec((SSM_GROUPS, None, tb // TOEP_CHUNK, SSM_GROUP, TOEP_CHUNK),
                              lambda i, j: (0, i, j, 0, 0))
    else:
        y_spec = pl.BlockSpec((nb, tb, SSM_WIDTH), lambda i, j: (i, j, 0))
    return pl.pallas_call(
        functools.partial(_layer_kernel, chunk, sub, plan.toeplitz),
        grid=(b // nb, t // tb),
        in_specs=[
            pl.BlockSpec((nb, tb, d), lambda i, j: (i, j, 0)),
            y_spec,
            row_spec, row_spec, row_spec, row_spec,
            state_spec,
            pl.BlockSpec((1, d), c2),
            pl.BlockSpec((d, RET_WIDTH), lambda i, j: (0, W_IN_G_RET_BLOCK)),
            pl.BlockSpec((d, SSM_WIDTH), lambda i, j: (0, W_IN_G_SSM_BLOCK)),
            pl.BlockSpec((1, RET_WIDTH), c2),
            pl.BlockSpec((RET_HEADS, chunk, chunk), c3),
            pl.BlockSpec((RET_HEADS, chunk, HEAD_DIM), c3),
            pl.BlockSpec((RET_HEADS, HEAD_DIM, HEAD_DIM), c3),
            pl.BlockSpec((SSM_WIDTH, SSM_WIDTH), c2),
            pl.BlockSpec((1, SSM_WIDTH), c2),
            pl.BlockSpec((D_MODEL, D_MODEL), c2),
            pl.BlockSpec((1, d), c2),
        ],
        out_specs=[
            pl.BlockSpec((nb, tb, d), lambda i, j: (i, j, 0)),
            state_spec,
        ],
        out_shape=[
            jax.ShapeDtypeStruct((b, t, d), F32),
            jax.ShapeDtypeStruct((b, RET_HEADS, HEAD_DIM, HEAD_DIM), F32),
        ],
        scratch_shapes=[pltpu.VMEM((rows, RET_WIDTH), F32)],
        compiler_params=pltpu.CompilerParams(
            dimension_semantics=("arbitrary", "arbitrary"), vmem_limit_bytes=VMEM_LIMIT_BYTES),
        name="retention_glu_layer",
    )(x, y, q, k, kdk, v, s0, norm_g, w_in, w_in, ret_norm_g, dm, qd, sd, w_glu, b_glu, w_out, final_g)


def _rotary_tables(pos0, steps, reps):
    f = np.float64
    half = HEAD_DIM // 2
    inv = f(ROPE_BASE) ** (-np.arange(half, dtype=f) / f(half))
    ang = (f(pos0) + np.arange(steps, dtype=f))[:, None] * inv[None, :]
    cos, sin = np.cos(ang), np.sin(ang)
    cos2 = np.concatenate([cos, cos], axis=-1)
    sin2 = np.concatenate([-sin, sin], axis=-1)
    return jnp.asarray(np.tile(cos2, (reps, 1)), F32), jnp.asarray(np.tile(sin2, (reps, 1)), F32)


def _retention_tables(chunk, sub):
    f = np.float64
    log_g = np.log1p(-np.power(f(2.0), f(-5.0) - np.arange(RET_HEADS, dtype=f)))
    idx = np.arange(chunk, dtype=f)
    diff = idx[:, None] - idx[None, :]
    causal = diff >= 0
    dm = np.where(causal[None], np.exp(np.where(causal, diff, f(0.0))[None] * log_g[:, None, None]), f(0.0))
    q_decay = np.exp((idx + f(1.0))[None, :] * log_g[:, None])
    k_decay = np.exp((f(chunk) - f(1.0) - idx)[None, :] * log_g[:, None])
    s_decay = np.exp(f(chunk) * log_g)
    qd = np.broadcast_to(q_decay[:, :, None], (RET_HEADS, chunk, HEAD_DIM))
    sd = np.broadcast_to(s_decay[:, None, None], (RET_HEADS, HEAD_DIM, HEAD_DIM))
    kd = np.broadcast_to(k_decay.T[:, :, None], (chunk, RET_HEADS, HEAD_DIM)).reshape(chunk, RET_WIDTH)
    kd_rows = np.tile(kd, (sub // chunk, 1))
    return tuple(jnp.asarray(np.ascontiguousarray(a), F32) for a in (dm, qd, sd, kd_rows))


def _discretise(lambda_re, lambda_im, log_step, b_re, b_im):
    a = jnp.minimum(lambda_re.astype(F32), LAMBDA_RE_MAX)
    b = lambda_im.astype(F32)
    dt = jnp.exp(log_step.astype(F32))[:, None]
    mag = jnp.exp(a * dt)
    lre, lim = mag * jnp.cos(b * dt), mag * jnp.sin(b * dt)
    den = a * a + b * b
    cr = ((lre - 1.0) * a + lim * b) / den
    ci = (lim * a - (lre - 1.0) * b) / den
    b_re, b_im = b_re.astype(F32), b_im.astype(F32)
    bre = cr[..., None] * b_re - ci[..., None] * b_im
    bim = cr[..., None] * b_im + ci[..., None] * b_re
    return (lre, lim), (bre, bim)


def _scan_tables(lam_bar, b_bar, c_re, c_im):
    gl = SSM_GROUPS // SSM_HALVES
    eye = jnp.eye(gl, dtype=F32)
    ar = lam_bar[0].reshape(SSM_HALVES, HALF_RI)
    ai = lam_bar[1].reshape(SSM_HALVES, HALF_RI)

    def place_b(m):
        m = m.reshape(SSM_HALVES, gl, SSM_STATE, SSM_GROUP)
        return jnp.einsum("hgpc,gk->hgckp", m, eye).reshape(SSM_HALVES, HALF_IN, HALF_RI)

    def place_c(m):
        m = m.reshape(SSM_HALVES, gl, SSM_GROUP, SSM_STATE)
        return jnp.einsum("hgcp,gk->hkpgc", m, eye).reshape(SSM_HALVES, HALF_RI, HALF_IN)

    bd = jnp.concatenate([place_b(b_bar[0]), place_b(b_bar[1])], axis=-1).astype(BF16)
    cd = jnp.concatenate([place_c(c_re.astype(F32)), place_c(-c_im.astype(F32))], axis=1).astype(BF16)
    return ar, ai, bd, cd


def _toeplitz_tables(lam_bar, b_bar, c_re, c_im, d):
    g, p = lam_bar[0].shape
    L = TOEP_CHUNK
    lre, lim = lam_bar
    bre, bim = b_bar
    cre, cim = c_re.astype(F32), c_im.astype(F32)
    lcr = jnp.broadcast_to(lre[:, :, None], (g, p, L))
    lci = jnp.broadcast_to(lim[:, :, None], (g, p, L))
    lrr = jnp.concatenate([lre, lre], axis=-1)[:, None, :]
    lri = jnp.concatenate([lim, lim], axis=-1)[:, None, :]
    cb_re = cre[:, None, :, :] * bre.transpose(0, 2, 1)[:, :, None, :] - cim[:, None, :, :] * bim.transpose(0, 2, 1)[:, :, None, :]
    cb_im = cre[:, None, :, :] * bim.transpose(0, 2, 1)[:, :, None, :] + cim[:, None, :, :] * bre.transpose(0, 2, 1)[:, :, None, :]
    cb = jnp.concatenate([cb_re, cb_im], axis=-1).reshape(g, SSM_GROUP * SSM_GROUP, 2 * p)
    bt_re, bt_im = bre.transpose(0, 2, 1), bim.transpose(0, 2, 1)
    bb1 = jnp.concatenate([bt_re, bt_im], axis=-1)
    bb2 = jnp.concatenate([-bt_im, bt_re], axis=-1)
    ct_re, ct_im = cre.transpose(0, 2, 1), cim.transpose(0, 2, 1)
    col_a = jnp.concatenate([ct_re, -ct_re], axis=1)
    col_b = jnp.concatenate([-ct_im, -ct_im], axis=1)
    ct = jnp.concatenate([col_a, col_b], axis=-1)
    dtab = jnp.broadcast_to(d.astype(F32).reshape(g, SSM_GROUP, 1), (g, SSM_GROUP, L))
    return lcr, lci, lrr, lri, cb, bb1, bb2, ct, dtab


def _state_to_lanes(re, im):
    b = re.shape[0]
    re = re.astype(F32).reshape(b, SSM_HALVES, HALF_RI)
    im = im.astype(F32).reshape(b, SSM_HALVES, HALF_RI)
    return jnp.concatenate([re, im], axis=-1).reshape(b, N_STATE)


def _lanes_to_state(h):
    b = h.shape[0]
    h = h.reshape(b, SSM_HALVES, 2, HALF_RI)
    return (h[:, :, 0].reshape(b, SSM_GROUPS, SSM_STATE), h[:, :, 1].reshape(b, SSM_GROUPS, SSM_STATE))


class _Plan(NamedTuple):
    nb: int
    tb: int
    chunk: int
    sub: int
    toeplitz: bool


def _plan(b, t):
    if t >= LAYER_ROWS:
        assert t % LAYER_ROWS == 0 and LAYER_ROWS % LAYER_CHUNK == 0 and LAYER_CHUNK % TOEP_CHUNK == 0
        return _Plan(nb=1, tb=LAYER_ROWS, chunk=LAYER_CHUNK, sub=LAYER_CHUNK, toeplitz=True)
    nb = min(b, LAYER_ROWS // t)
    assert b % nb == 0
    return _Plan(nb=nb, tb=t, chunk=t, sub=nb * t, toeplitz=False)


def _stream_step(x, pos0, ret_s0, h0_re, h0_im, weights):
    (norm_g, w_in, w_ut, ret_norm_g, lam_bar, b_bar, c_re, c_im, ssm_d, w_glu, b_glu, w_out,
     final_g) = weights
    b, t, _ = x.shape
    plan = _plan(b, t)
    cos, sin = _rotary_tables(pos0, t, plan.nb)
    dm, qd, sd, kd_rows = _retention_tables(plan.chunk, plan.sub)
    front = _front_call(x, norm_g, w_ut if plan.toeplitz else None, w_in, cos, sin, kd_rows,
                        plan.nb, plan.tb, plan.sub, plan.toeplitz)
    if plan.toeplitz:
        ut, q, k, kdk, v = front
        h0 = jnp.concatenate([h0_re, h0_im], axis=-1).astype(F32).transpose(1, 0, 2)
        y, h_fin = _toeplitz_call(ut, h0, _toeplitz_tables(lam_bar, b_bar, c_re, c_im, ssm_d))
        h_fin = h_fin.transpose(1, 0, 2)
        h_re, h_im = h_fin[..., :SSM_STATE], h_fin[..., SSM_STATE:]
    else:
        q, k, kdk, v = front
        ar, ai, bd, cd = _scan_tables(lam_bar, b_bar, c_re, c_im)
        y, h_fin = _ssm_call(x, norm_g, w_in, _state_to_lanes(h0_re, h0_im), ar, ai, bd, cd,
                             ssm_d[None, :].astype(F32), plan.tb)
        h_re, h_im = _lanes_to_state(h_fin)
    out, s_fin = _layer_call(x, y, q, k, kdk, v, ret_s0, norm_g, w_in, ret_norm_g, dm, qd, sd,
                             w_glu, b_glu, w_out, final_g, plan)
    return out, s_fin, h_re, h_im


def kernel(x_prompt, x_sample, state_ret, state_ssm_re, state_ssm_im, norm_g, w_in, ret_norm_g,
           ssm_lambda_re, ssm_lambda_im, ssm_log_step, ssm_b_re, ssm_b_im, ssm_c_re, ssm_c_im,
           ssm_d, w_glu, b_glu, w_out, final_norm_g):
    depth = w_in.shape[0]
    assert depth == 1, "single-layer trunk"
    bp, tp, _ = x_prompt.shape
    bs, ts, _ = x_sample.shape
    l = 0
    w = w_in[l].astype(BF16)
    w_ut = w[:, W_IN_U_BLOCK * SSM_WIDTH:(W_IN_U_BLOCK + 1) * SSM_WIDTH].T
    lam_bar, b_bar = _discretise(ssm_lambda_re[l], ssm_lambda_im[l], ssm_log_step[l], ssm_b_re[l], ssm_b_im[l])
    weights = (
        norm_g[l][None, :].astype(F32), w, w_ut, ret_norm_g[l].reshape(1, RET_WIDTH).astype(F32),
        lam_bar, b_bar, ssm_c_re[l], ssm_c_im[l], ssm_d[l],
        w_glu[l].astype(BF16), b_glu[l][None, :].astype(F32),
        w_out[l].astype(BF16), final_norm_g[None, :].astype(F32),
    )

    ret0_p = jnp.zeros((bp, RET_HEADS, HEAD_DIM, HEAD_DIM), F32)
    h0_p = jnp.zeros((bp, SSM_GROUPS, SSM_STATE), F32)
    y_p, r_p, hp_re, hp_im = _stream_step(x_prompt, 0, ret0_p, h0_p, h0_p, weights)
    y_s, r_s, hs_re, hs_im = _stream_step(x_sample, PAST_LEN, state_ret[l].astype(F32),
                                          state_ssm_re[l], state_ssm_im[l], weights)
    return (y_p, y_s, r_p[None], hp_re[None], hp_im[None], r_s[None], hs_re[None], hs_im[None])
```

```python
import functools
from typing import NamedTuple

import jax
import jax.numpy as jnp
import numpy as np
from jax import lax
from jax.experimental import pallas as pl
from jax.experimental.pallas import tpu as pltpu

D_MODEL = 1024
RET_HEADS = 4
HEAD_DIM = 128
RET_WIDTH = RET_HEADS * HEAD_DIM
SSM_WIDTH = 512
SSM_GROUP = 16
SSM_GROUPS = 32
SSM_STATE = 64
ROPE_BASE = 10000.0
EPS = 1e-6
LAMBDA_RE_MAX = -1e-4
PAST_LEN = 2048

LAYER_ROWS = 1024
LAYER_CHUNK = 256
TOEP_CHUNK = 128
TOEP_LANES = SSM_GROUP * TOEP_CHUNK

W_IN_G_RET_BLOCK = 3
W_IN_U_BLOCK = 4
W_IN_G_SSM_BLOCK = 5

SSM_HALVES = 2
HALF_IN = SSM_WIDTH // SSM_HALVES
HALF_RI = (SSM_GROUPS // SSM_HALVES) * SSM_STATE
HALF_STATE = 2 * HALF_RI
N_STATE = SSM_HALVES * HALF_STATE
SCAN_LANES = 512

VMEM_LIMIT_BYTES = 56 * 1024 * 1024

F32 = jnp.float32
BF16 = jnp.bfloat16


def _rms(x, g):
    return x * lax.rsqrt(jnp.mean(x * x, axis=-1, keepdims=True) + EPS) * g


def _ssm_kernel(x_ref, g_ref, wu_ref, h0_ref, ar_ref, ai_ref, bd_ref, cd_ref, d_ref, y_ref, hfin_ref,
                ubt_ref, utb_ref, bu_ref, hs_ref, ytb_ref):
    nb, tb, d_model = x_ref.shape
    w = SSM_WIDTH
    rows = nb * tb

    @pl.when(pl.program_id(0) == 0)
    def _():
        hfin_ref[...] = h0_ref[...]

    hn = _rms(x_ref[...].reshape(rows, d_model), g_ref[...]).astype(BF16)
    ubt_ref[...] = jnp.dot(hn, wu_ref[...], preferred_element_type=F32).reshape(nb, tb, w)
    for t in range(tb):
        utb_ref[t] = ubt_ref[:, t, :]
    u = utb_ref[...].reshape(rows, w)
    ub = u.astype(BF16)

    for h in range(SSM_HALVES):
        cols = slice(h * HALF_IN, (h + 1) * HALF_IN)
        yh = d_ref[:, cols] * u[:, cols]
        for q in range(HALF_RI // SCAN_LANES):
            re_h = q * SCAN_LANES
            im_h = HALF_RI + re_h
            re = h * HALF_STATE + re_h
            im = h * HALF_STATE + im_h
            bu_ref[:, re:re + SCAN_LANES] = jnp.dot(
                ub[:, cols], bd_ref[h, :, re_h:re_h + SCAN_LANES], preferred_element_type=F32)
            bu_ref[:, im:im + SCAN_LANES] = jnp.dot(
                ub[:, cols], bd_ref[h, :, im_h:im_h + SCAN_LANES], preferred_element_type=F32)
            ar = jnp.broadcast_to(ar_ref[h:h + 1, re_h:re_h + SCAN_LANES], (nb, SCAN_LANES))
            ai = jnp.broadcast_to(ai_ref[h:h + 1, re_h:re_h + SCAN_LANES], (nb, SCAN_LANES))
            hr = hfin_ref[:, re:re + SCAN_LANES]
            hi = hfin_ref[:, im:im + SCAN_LANES]
            for t in range(tb):
                r = slice(t * nb, (t + 1) * nb)
                hr, hi = (ar * hr - ai * hi + bu_ref[r, re:re + SCAN_LANES],
                          ar * hi + ai * hr + bu_ref[r, im:im + SCAN_LANES])
                hs_ref[r, re:re + SCAN_LANES] = hr.astype(BF16)
                hs_ref[r, im:im + SCAN_LANES] = hi.astype(BF16)
            hfin_ref[:, re:re + SCAN_LANES] = hr
            hfin_ref[:, im:im + SCAN_LANES] = hi
            yh = yh + jnp.dot(hs_ref[:, re:re + SCAN_LANES], cd_ref[h, re_h:re_h + SCAN_LANES, :],
                              preferred_element_type=F32)
            yh = yh + jnp.dot(hs_ref[:, im:im + SCAN_LANES], cd_ref[h, im_h:im_h + SCAN_LANES, :],
                              preferred_element_type=F32)
        ytb_ref[:, :, cols] = yh.reshape(tb, nb, HALF_IN)
    for b in range(nb):
        y_ref[b] = ytb_ref[:, b, :]


def _front_kernel(emit_ut, sub, x_ref, g_ref, *refs):
    if emit_ut:
        wut_ref, wqkv_ref, cos_ref, sin_ref, kd_ref, ut_ref, q_ref, k_ref, kdk_ref, v_ref = refs
    else:
        wqkv_ref, cos_ref, sin_ref, kd_ref, q_ref, k_ref, kdk_ref, v_ref = refs
    nb, tb, d = x_ref.shape
    rows = nb * tb

    def load_rows(ref, r0):
        if nb == 1:
            return ref[0, r0:r0 + sub, :]
        return ref[r0 // tb:(r0 + sub) // tb].reshape(sub, ref.shape[-1])

    def store_rows(ref, r0, lanes, val):
        if nb == 1:
            ref[0, r0:r0 + sub, lanes] = val
        else:
            ref[r0 // tb:(r0 + sub) // tb, :, lanes] = val.reshape(sub // tb, tb, val.shape[-1])

    hns = []
    for r0 in range(0, rows, sub):
        hn = _rms(load_rows(x_ref, r0), g_ref[...]).astype(BF16)
        hns.append(hn)
        qkv = jnp.dot(hn, wqkv_ref[...], preferred_element_type=F32)
        cos = cos_ref[r0:r0 + sub, :]
        sin = sin_ref[r0:r0 + sub, :]

        def rotary(a):
            return a * cos + pltpu.roll(a, HEAD_DIM // 2, 1) * sin

        for h in range(RET_HEADS):
            lanes = slice(h * HEAD_DIM, (h + 1) * HEAD_DIM)
            k = rotary(qkv[:, RET_WIDTH + h * HEAD_DIM:RET_WIDTH + (h + 1) * HEAD_DIM]) * (HEAD_DIM ** -0.5)
            store_rows(q_ref, r0, lanes, rotary(qkv[:, lanes]).astype(BF16))
            store_rows(k_ref, r0, lanes, k.astype(BF16))
            store_rows(kdk_ref, r0, lanes, (k * kd_ref[:, lanes]).astype(BF16))
            store_rows(v_ref, r0, lanes,
                       qkv[:, 2 * RET_WIDTH + h * HEAD_DIM:2 * RET_WIDTH + (h + 1) * HEAD_DIM].astype(BF16))

    if emit_ut:
        nk = rows // TOEP_CHUNK
        hn = jnp.concatenate(hns, axis=0)
        ut = lax.dot_general(wut_ref[...], hn, (((1,), (1,)), ((), ())), preferred_element_type=F32)
        for g in range(SSM_GROUPS):
            grp = slice(g * SSM_GROUP, (g + 1) * SSM_GROUP)
            slabs = jnp.stack([ut[grp, kk * TOEP_CHUNK:(kk + 1) * TOEP_CHUNK] for kk in range(nk)])
            by_channel = jnp.swapaxes(slabs, 0, 1)
            for c in range(SSM_GROUP):
                ut_ref[g, :, c * TOEP_CHUNK:(c + 1) * TOEP_CHUNK] = by_channel[c]


def _front_call(x, norm_g, w_ut, w_in, cos, sin, kd_rows, nb, tb, sub):
    b, t, d = x.shape
    rows = nb * tb
    c2 = lambda i, j: (0, 0)
    row_spec = pl.BlockSpec((nb, tb, RET_WIDTH), lambda i, j: (i, j, 0))
    return pl.pallas_call(
        functools.partial(_front_kernel, True, sub),
        grid=(b // nb, t // tb),
        in_specs=[
            pl.BlockSpec((nb, tb, d), lambda i, j: (i, j, 0)),
            pl.BlockSpec((1, d), c2),
            pl.BlockSpec((SSM_WIDTH, d), c2),
            pl.BlockSpec((d, 3 * RET_WIDTH), c2),
            pl.BlockSpec((rows, HEAD_DIM), lambda i, j: (j, 0)),
            pl.BlockSpec((rows, HEAD_DIM), lambda i, j: (j, 0)),
            pl.BlockSpec((sub, RET_WIDTH), c2),
        ],
        out_specs=[pl.BlockSpec((SSM_GROUPS, None, tb // TOEP_CHUNK, TOEP_LANES), lambda i, j: (0, i, j, 0))]
        + [row_spec] * 4,
        out_shape=[jax.ShapeDtypeStruct((SSM_GROUPS, b, t // TOEP_CHUNK, TOEP_LANES), F32)]
        + [jax.ShapeDtypeStruct((b, t, RET_WIDTH), BF16)] * 4,
        compiler_params=pltpu.CompilerParams(
            dimension_semantics=("arbitrary", "arbitrary"), vmem_limit_bytes=VMEM_LIMIT_BYTES),
        name="qkv_rotary_front",
    )(x, norm_g, w_ut, w_in, cos, sin, kd_rows)


def _powers(base_re, base_im, exponent, n_bits):
    pr = jnp.ones_like(base_re)
    pi = jnp.zeros_like(base_re)
    br, bi = base_re, base_im
    for bit in range(n_bits):
        take = ((exponent >> bit) & 1) == 1
        pr, pi = jnp.where(take, pr * br - pi * bi, pr), jnp.where(take, pr * bi + pi * br, pi)
        br, bi = br * br - bi * bi, 2.0 * br * bi
    return pr, pi


def _split_bf16(x):
    hi = x.astype(BF16)
    r = x - hi.astype(F32)
    mid = r.astype(BF16)
    lo = (r - mid.astype(F32)).astype(BF16)
    return hi, mid, lo


def _dot_f32(a, b):
    a3, b3 = _split_bf16(a), _split_bf16(b)
    acc = None
    for i in range(3):
        for j in range(3 - i):
            term = jnp.dot(a3[i], b3[j], preferred_element_type=F32)
            acc = term if acc is None else acc + term
    return acc


def _toeplitz_kernel(u_ref, h0_ref, lcr_ref, lci_ref, lrr_ref, lri_ref, cb_ref, bb1_ref, bb2_ref,
                     ct_ref, d_ref, y_ref, hfin_ref, a_ref, wb_ref, wc_ref, t_ref, ys_ref, yi_ref):
    nb, nk, _ = u_ref.shape
    rows = nb * nk
    L = TOEP_CHUNK
    n_bits = L.bit_length() - 1
    sub_i = lax.broadcasted_iota(jnp.int32, (L, L), 0)
    lane_i = lax.broadcasted_iota(jnp.int32, (L, L), 1)

    lcr, lci = lcr_ref[...], lci_ref[...]
    lane_p = lax.broadcasted_iota(jnp.int32, lcr.shape, 1)
    p0r, p0i = _powers(lcr, lci, lane_p, n_bits)
    p1r, p1i = p0r * lcr - p0i * lci, p0r * lci + p0i * lcr

    taps = _dot_f32(cb_ref[...], jnp.concatenate([p0r, -p0i], axis=0))

    lrr = jnp.broadcast_to(lrr_ref[...], (L, L))
    lri = jnp.broadcast_to(lri_ref[...], (L, L))
    qr, qi = _powers(lrr, lri, (L - 1) - sub_i, n_bits)
    for c in range(SSM_GROUP):
        wb_ref[c * L:(c + 1) * L, :] = (qr * bb1_ref[c:c + 1, :] + qi * bb2_ref[c:c + 1, :]).astype(BF16)
    pa = jnp.concatenate([p1r, p1i], axis=0)
    pb = jnp.concatenate([p1i, p1r], axis=0)
    for c in range(SSM_GROUP):
        wc_ref[:, c * L:(c + 1) * L] = (
            jnp.broadcast_to(ct_ref[:, c:c + 1], (L, L)) * pa
            + jnp.broadcast_to(ct_ref[:, SSM_GROUP + c:SSM_GROUP + c + 1], (L, L)) * pb).astype(BF16)

    a_ref[...] = u_ref[...].reshape(rows, TOEP_LANES).astype(BF16)

    s_in = jnp.dot(a_ref[...], wb_ref[...], preferred_element_type=F32)
    s_in = jnp.swapaxes(s_in.reshape(nb, nk, L), 0, 1)
    l128r = qr[0:1] * lrr[0:1] - qi[0:1] * lri[0:1]
    l128i = qr[0:1] * lri[0:1] + qi[0:1] * lrr[0:1]
    half = L // 2
    is_re = lane_i[0:1] < half
    mul_same = jnp.broadcast_to(l128r, (nb, L))
    mul_swap = jnp.broadcast_to(jnp.where(is_re, -l128i, l128i), (nb, L))
    s_sw = pltpu.roll(s_in.reshape(nk * nb, L), half, 1).reshape(nk, nb, L)
    h = h0_ref[...]
    h_sw = pltpu.roll(h, half, 1)
    h_prev = []
    for k in range(nk):
        h_prev.append(h)
        h, h_sw = (h * mul_same + h_sw * mul_swap + s_in[k],
                   h_sw * mul_same - h * mul_swap + s_sw[k])
    hfin_ref[...] = h
    h_prev = jnp.swapaxes(jnp.stack(h_prev), 0, 1).reshape(rows, L).astype(BF16)

    causal = lane_i >= sub_i
    for pair in range(SSM_GROUP // 2):
        slot = pair
        for ci in range(SSM_GROUP):
            for cc in range(2):
                r = ci * SSM_GROUP + 2 * pair + cc
                tap = jnp.broadcast_to(taps[r:r + 1, :], (L, L))
                toep = jnp.where(causal, pltpu.roll(tap, 0, 1, stride=1, stride_axis=0), 0.0)
                t_ref[slot, ci * L:(ci + 1) * L, cc * L:(cc + 1) * L] = toep.astype(BF16)
        yv = jnp.dot(a_ref[...], t_ref[slot], preferred_element_type=F32)
        for cc in range(2):
            c = 2 * pair + cc
            uc = u_ref[:, :, c * L:(c + 1) * L]
            ys_ref[c] = yv[:, cc * L:(cc + 1) * L].reshape(nb, nk, L) + d_ref[c:c + 1, :] * uc
    inter = jnp.dot(h_prev, wc_ref[...], preferred_element_type=F32)
    for c in range(SSM_GROUP):
        yi_ref[c] = inter[:, c * L:(c + 1) * L].reshape(nb, nk, L)
    for b in range(nb):
        y_ref[b] = jnp.swapaxes(ys_ref[:, b] + yi_ref[:, b], 0, 1)


def _toeplitz_call(ut, h0, tabs):
    g, nb, nk, lanes = ut.shape
    L = TOEP_CHUNK
    rows = nb * nk
    lcr, lci, lrr, lri, cb, bb1, bb2, ct, dtab = tabs

    def per_group(*shape):
        return pl.BlockSpec((None,) + shape, lambda i: (i,) + (0,) * len(shape))

    return pl.pallas_call(
        _toeplitz_kernel,
        grid=(g,),
        in_specs=[
            per_group(nb, nk, lanes), per_group(nb, L),
            per_group(SSM_STATE, L), per_group(SSM_STATE, L), per_group(1, L), per_group(1, L),
            per_group(SSM_GROUP * SSM_GROUP, L), per_group(SSM_GROUP, L), per_group(SSM_GROUP, L),
            per_group(L, 2 * SSM_GROUP), per_group(SSM_GROUP, L),
        ],
        out_specs=[per_group(nb, nk, SSM_GROUP, L), per_group(nb, L)],
        out_shape=[
            jax.ShapeDtypeStruct((g, nb, nk, SSM_GROUP, L), F32),
            jax.ShapeDtypeStruct((g, nb, L), F32),
        ],
        scratch_shapes=[
            pltpu.VMEM((rows, lanes), BF16),
            pltpu.VMEM((lanes, L), BF16),
            pltpu.VMEM((L, lanes), BF16),
            pltpu.VMEM((SSM_GROUP // 2, lanes, 2 * L), BF16),
            pltpu.VMEM((SSM_GROUP, nb, nk, L), F32),
            pltpu.VMEM((SSM_GROUP, nb, nk, L), F32),
        ],
        compiler_params=pltpu.CompilerParams(
            dimension_semantics=("arbitrary",), vmem_limit_bytes=VMEM_LIMIT_BYTES),
        name="ssm_toeplitz",
    )(ut, h0, lcr, lci, lrr, lri, cb, bb1, bb2, ct, dtab)


def _layer_kernel(chunk, sub, y_by_group, x_ref, y_ref, q_ref, k_ref, kdk_ref, v_ref, s0_ref, ng_ref,
                  wgr_ref, wgs_ref, rg_ref, dm_ref, qd_ref, sd_ref, wglu_ref, bglu_ref, wo_ref, fg_ref,
                  out_ref, sfin_ref, o_ref):
    nb, tb, d = x_ref.shape
    rows = nb * tb

    @pl.when(pl.program_id(1) == 0)
    def _():
        sfin_ref[...] = s0_ref[...]

    def load_rows(ref, r0, lanes=slice(None)):
        if nb == 1:
            return ref[0, r0:r0 + sub, lanes]
        val = ref[r0 // tb:(r0 + sub) // tb, :, lanes]
        return val.reshape(sub, val.shape[-1])

    def store_rows(ref, r0, val):
        if nb == 1:
            ref[0, r0:r0 + sub, :] = val
        else:
            ref[r0 // tb:(r0 + sub) // tb] = val.reshape(sub // tb, tb, ref.shape[-1])

    rg = rg_ref[...]
    for r0 in range(0, rows, sub):
        x = load_rows(x_ref, r0)
        hn = _rms(x, ng_ref[...]).astype(BF16)
        g_ret = jnp.dot(hn, wgr_ref[...], preferred_element_type=F32)
        g_ssm = jnp.dot(hn, wgs_ref[...], preferred_element_type=F32)

        for h in range(RET_HEADS):
            lanes = slice(h * HEAD_DIM, (h + 1) * HEAD_DIM)
            q, kb, kdk, vb = (load_rows(ref, r0, lanes) for ref in (q_ref, k_ref, kdk_ref, v_ref))
            dm = dm_ref[h]
            qd = qd_ref[h]
            sd = sd_ref[h]
            for c0 in range(0, sub, chunk):
                n = (r0 + c0) // tb
                r = slice(c0, c0 + chunk)
                qc, kc, vc = q[r], kb[r], vb[r]
                s = sfin_ref[n, h]
                scores = lax.dot_general(qc, kc, (((1,), (1,)), ((), ())), preferred_element_type=F32)
                p = (scores * dm).astype(BF16)
                o = jnp.dot(p, vc, preferred_element_type=F32)
                o = o + jnp.dot(qc, s.astype(BF16), preferred_element_type=F32) * qd
                sfin_ref[n, h] = sd * s + jnp.dot(kdk[r].T, vc, preferred_element_type=F32)
                o_ref[r0 + c0:r0 + c0 + chunk, lanes] = o

        o_parts = []
        for h in range(RET_HEADS):
            lanes = slice(h * HEAD_DIM, (h + 1) * HEAD_DIM)
            o_parts.append(_rms(o_ref[r0:r0 + sub, lanes], rg[:, lanes]))
        o = jnp.concatenate(o_parts, axis=-1) * jax.nn.silu(g_ret)

        if y_by_group:
            k0 = r0 // TOEP_CHUNK
            y = jnp.concatenate(
                [y_ref[:, k0 + j].reshape(SSM_WIDTH, TOEP_CHUNK).T for j in range(sub // TOEP_CHUNK)], axis=0)
        else:
            y = load_rows(y_ref, r0)
        z = jax.nn.gelu(y)
        gate = jnp.dot(z.astype(BF16), wglu_ref[...], preferred_element_type=F32) + bglu_ref[...]
        z = z * jax.nn.sigmoid(gate)
        z = z * jax.nn.silu(g_ssm)

        mix = jnp.concatenate([o, z], axis=-1).astype(BF16)
        res = x + jnp.dot(mix, wo_ref[...], preferred_element_type=F32)
        store_rows(out_ref, r0, _rms(res, fg_ref[...]))


def _layer_call(x, y, q, k, kdk, v, s0, norm_g, w_in, ret_norm_g, dm, qd, sd, w_glu, b_glu,
                w_out, final_g, plan):
    b, t, d = x.shape
    nb, tb, chunk, sub = plan.nb, plan.tb, plan.chunk, plan.sub
    rows = nb * tb
    c2 = lambda i, j: (0, 0)
    c3 = lambda i, j: (0, 0, 0)
    state_spec = pl.BlockSpec((nb, RET_HEADS, HEAD_DIM, HEAD_DIM), lambda i, j: (i, 0, 0, 0))
    row_spec = pl.BlockSpec((nb, tb, RET_WIDTH), lambda i, j: (i, j, 0))
    y_spec = pl.BlockSpec((SSM_GROUPS, None, tb // TOEP_CHUNK, SSM_GROUP, TOEP_CHUNK),
                          lambda i, j: (0, i, j, 0, 0))
    return pl.pallas_call(
        functools.partial(_layer_kernel, chunk, sub, True),
        grid=(b // nb, t // tb),
        in_specs=[
            pl.BlockSpec((nb, tb, d), lambda i, j: (i, j, 0)),
            y_spec,
            row_spec, row_spec, row_spec, row_spec,
            state_spec,
            pl.BlockSpec((1, d), c2),
            pl.BlockSpec((d, RET_WIDTH), lambda i, j: (0, W_IN_G_RET_BLOCK)),
            pl.BlockSpec((d, SSM_WIDTH), lambda i, j: (0, W_IN_G_SSM_BLOCK)),
            pl.BlockSpec((1, RET_WIDTH), c2),
            pl.BlockSpec((RET_HEADS, chunk, chunk), c3),
            pl.BlockSpec((RET_HEADS, chunk, HEAD_DIM), c3),
            pl.BlockSpec((RET_HEADS, HEAD_DIM, HEAD_DIM), c3),
            pl.BlockSpec((SSM_WIDTH, SSM_WIDTH), c2),
            pl.BlockSpec((1, SSM_WIDTH), c2),
            pl.BlockSpec((D_MODEL, D_MODEL), c2),
            pl.BlockSpec((1, d), c2),
        ],
        out_specs=[
            pl.BlockSpec((nb, tb, d), lambda i, j: (i, j, 0)),
            state_spec,
        ],
        out_shape=[
            jax.ShapeDtypeStruct((b, t, d), F32),
            jax.ShapeDtypeStruct((b, RET_HEADS, HEAD_DIM, HEAD_DIM), F32),
        ],
        scratch_shapes=[pltpu.VMEM((rows, RET_WIDTH), F32)],
        compiler_params=pltpu.CompilerParams(
            dimension_semantics=("arbitrary", "arbitrary"), vmem_limit_bytes=VMEM_LIMIT_BYTES),
        name="retention_glu_layer",
    )(x, y, q, k, kdk, v, s0, norm_g, w_in, w_in, ret_norm_g, dm, qd, sd, w_glu, b_glu, w_out, final_g)


def _short_kernel(chunk, sub, x_ref, ng_ref, wqkv_ref, cos_ref, sin_ref, kd_ref, wu_ref, h0_ref, ar_ref,
                  ai_ref, bd_ref, cd_ref, d_ref, s0_ref, wgr_ref, wgs_ref, rg_ref, dm_ref, qd_ref, sd_ref,
                  wglu_ref, bglu_ref, wo_ref, fg_ref, out_ref, sfin_ref, hfin_ref,
                  q_ref, k_ref, kdk_ref, v_ref, y_ref, ubt_ref, utb_ref, bu_ref, hs_ref, ytb_ref, o_ref):
    _front_kernel(False, sub, x_ref, ng_ref, wqkv_ref, cos_ref, sin_ref, kd_ref, q_ref, k_ref, kdk_ref, v_ref)
    _ssm_kernel(x_ref, ng_ref, wu_ref, h0_ref, ar_ref, ai_ref, bd_ref, cd_ref, d_ref, y_ref, hfin_ref,
                ubt_ref, utb_ref, bu_ref, hs_ref, ytb_ref)
    _layer_kernel(chunk, sub, False, x_ref, y_ref, q_ref, k_ref, kdk_ref, v_ref, s0_ref, ng_ref, wgr_ref,
                  wgs_ref, rg_ref, dm_ref, qd_ref, sd_ref, wglu_ref, bglu_ref, wo_ref, fg_ref,
                  out_ref, sfin_ref, o_ref)


def _short_call(x, norm_g, w_in, cos, sin, kd_rows, h0, ar, ai, bd, cd, d, s0, ret_norm_g, dm, qd, sd,
                w_glu, b_glu, w_out, final_g, plan):
    b, t, dm_ = x.shape
    chunk, sub = plan.chunk, plan.sub
    assert plan.nb == b and plan.tb == t, "short streams are handled as one block"
    rows = b * t
    c2 = lambda i, j: (0, 0)
    c3 = lambda i, j: (0, 0, 0)
    c4 = lambda i, j: (0, 0, 0, 0)
    full3 = lambda w: pl.BlockSpec((b, t, w), c3)
    state_spec = pl.BlockSpec((b, RET_HEADS, HEAD_DIM, HEAD_DIM), c4)
    return pl.pallas_call(
        functools.partial(_short_kernel, chunk, sub),
        grid=(1, 1),
        in_specs=[
            full3(dm_),
            pl.BlockSpec((1, dm_), c2),
            pl.BlockSpec((dm_, 3 * RET_WIDTH), c2),
            pl.BlockSpec((rows, HEAD_DIM), c2),
            pl.BlockSpec((rows, HEAD_DIM), c2),
            pl.BlockSpec((sub, RET_WIDTH), c2),
            pl.BlockSpec((dm_, SSM_WIDTH), lambda i, j: (0, W_IN_U_BLOCK)),
            pl.BlockSpec((b, N_STATE), c2),
            pl.BlockSpec((SSM_HALVES, HALF_RI), c2),
            pl.BlockSpec((SSM_HALVES, HALF_RI), c2),
            pl.BlockSpec((SSM_HALVES, HALF_IN, HALF_STATE), c3),
            pl.BlockSpec((SSM_HALVES, HALF_STATE, HALF_IN), c3),
            pl.BlockSpec((1, SSM_WIDTH), c2),
            state_spec,
            pl.BlockSpec((dm_, RET_WIDTH), lambda i, j: (0, W_IN_G_RET_BLOCK)),
            pl.BlockSpec((dm_, SSM_WIDTH), lambda i, j: (0, W_IN_G_SSM_BLOCK)),
            pl.BlockSpec((1, RET_WIDTH), c2),
            pl.BlockSpec((RET_HEADS, chunk, chunk), c3),
            pl.BlockSpec((RET_HEADS, chunk, HEAD_DIM), c3),
            pl.BlockSpec((RET_HEADS, HEAD_DIM, HEAD_DIM), c3),
            pl.BlockSpec((SSM_WIDTH, SSM_WIDTH), c2),
            pl.BlockSpec((1, SSM_WIDTH), c2),
            pl.BlockSpec((D_MODEL, D_MODEL), c2),
            pl.BlockSpec((1, dm_), c2),
        ],
        out_specs=[full3(dm_), state_spec, pl.BlockSpec((b, N_STATE), c2)],
        out_shape=[
            jax.ShapeDtypeStruct((b, t, dm_), F32),
            jax.ShapeDtypeStruct((b, RET_HEADS, HEAD_DIM, HEAD_DIM), F32),
            jax.ShapeDtypeStruct((b, N_STATE), F32),
        ],
        scratch_shapes=[pltpu.VMEM((b, t, RET_WIDTH), BF16)] * 4 + [
            pltpu.VMEM((b, t, SSM_WIDTH), F32),
            pltpu.VMEM((b, t, SSM_WIDTH), F32),
            pltpu.VMEM((t, b, SSM_WIDTH), F32),
            pltpu.VMEM((rows, N_STATE), F32),
            pltpu.VMEM((rows, N_STATE), BF16),
            pltpu.VMEM((t, b, SSM_WIDTH), F32),
            pltpu.VMEM((rows, RET_WIDTH), F32),
        ],
        compiler_params=pltpu.CompilerParams(
            dimension_semantics=("arbitrary", "arbitrary"), vmem_limit_bytes=VMEM_LIMIT_BYTES),
        name="short_stream_step",
    )(x, norm_g, w_in, cos, sin, kd_rows, w_in, h0, ar, ai, bd, cd, d, s0, w_in, w_in, ret_norm_g, dm, qd, sd,
      w_glu, b_glu, w_out, final_g)


def _rotary_tables(pos0, steps, reps):
    f = np.float64
    half = HEAD_DIM // 2
    inv = f(ROPE_BASE) ** (-np.arange(half, dtype=f) / f(half))
    ang = (f(pos0) + np.arange(steps, dtype=f))[:, None] * inv[None, :]
    cos, sin = np.cos(ang), np.sin(ang)
    cos2 = np.concatenate([cos, cos], axis=-1)
    sin2 = np.concatenate([-sin, sin], axis=-1)
    return jnp.asarray(np.tile(cos2, (reps, 1)), F32), jnp.asarray(np.tile(sin2, (reps, 1)), F32)


def _retention_tables(chunk, sub):
    f = np.float64
    log_g = np.log1p(-np.power(f(2.0), f(-5.0) - np.arange(RET_HEADS, dtype=f)))
    idx = np.arange(chunk, dtype=f)
    diff = idx[:, None] - idx[None, :]
    causal = diff >= 0
    dm = np.where(causal[None], np.exp(np.where(causal, diff, f(0.0))[None] * log_g[:, None, None]), f(0.0))
    q_decay = np.exp((idx + f(1.0))[None, :] * log_g[:, None])
    k_decay = np.exp((f(chunk) - f(1.0) - idx)[None, :] * log_g[:, None])
    s_decay = np.exp(f(chunk) * log_g)
    qd = np.broadcast_to(q_decay[:, :, None], (RET_HEADS, chunk, HEAD_DIM))
    sd = np.broadcast_to(s_decay[:, None, None], (RET_HEADS, HEAD_DIM, HEAD_DIM))
    kd = np.broadcast_to(k_decay.T[:, :, None], (chunk, RET_HEADS, HEAD_DIM)).reshape(chunk, RET_WIDTH)
    kd_rows = np.tile(kd, (sub // chunk, 1))
    return tuple(jnp.asarray(np.ascontiguousarray(a), F32) for a in (dm, qd, sd, kd_rows))


def _discretise(lambda_re, lambda_im, log_step, b_re, b_im):
    a = jnp.minimum(lambda_re.astype(F32), LAMBDA_RE_MAX)
    b = lambda_im.astype(F32)
    dt = jnp.exp(log_step.astype(F32))[:, None]
    mag = jnp.exp(a * dt)
    lre, lim = mag * jnp.cos(b * dt), mag * jnp.sin(b * dt)
    den = a * a + b * b
    cr = ((lre - 1.0) * a + lim * b) / den
    ci = (lim * a - (lre - 1.0) * b) / den
    b_re, b_im = b_re.astype(F32), b_im.astype(F32)
    bre = cr[..., None] * b_re - ci[..., None] * b_im
    bim = cr[..., None] * b_im + ci[..., None] * b_re
    return (lre, lim), (bre, bim)


def _scan_tables(lam_bar, b_bar, c_re, c_im):
    gl = SSM_GROUPS // SSM_HALVES
    eye = jnp.eye(gl, dtype=F32)
    ar = lam_bar[0].reshape(SSM_HALVES, HALF_RI)
    ai = lam_bar[1].reshape(SSM_HALVES, HALF_RI)

    def place_b(m):
        m = m.reshape(SSM_HALVES, gl, SSM_STATE, SSM_GROUP)
        return jnp.einsum("hgpc,gk->hgckp", m, eye).reshape(SSM_HALVES, HALF_IN, HALF_RI)

    def place_c(m):
        m = m.reshape(SSM_HALVES, gl, SSM_GROUP, SSM_STATE)
        return jnp.einsum("hgcp,gk->hkpgc", m, eye).reshape(SSM_HALVES, HALF_RI, HALF_IN)

    bd = jnp.concatenate([place_b(b_bar[0]), place_b(b_bar[1])], axis=-1).astype(BF16)
    cd = jnp.concatenate([place_c(c_re.astype(F32)), place_c(-c_im.astype(F32))], axis=1).astype(BF16)
    return ar, ai, bd, cd


def _toeplitz_tables(lam_bar, b_bar, c_re, c_im, d):
    g, p = lam_bar[0].shape
    L = TOEP_CHUNK
    lre, lim = lam_bar
    bre, bim = b_bar
    cre, cim = c_re.astype(F32), c_im.astype(F32)
    lcr = jnp.broadcast_to(lre[:, :, None], (g, p, L))
    lci = jnp.broadcast_to(lim[:, :, None], (g, p, L))
    lrr = jnp.concatenate([lre, lre], axis=-1)[:, None, :]
    lri = jnp.concatenate([lim, lim], axis=-1)[:, None, :]
    cb_re = cre[:, None, :, :] * bre.transpose(0, 2, 1)[:, :, None, :] - cim[:, None, :, :] * bim.transpose(0, 2, 1)[:, :, None, :]
    cb_im = cre[:, None, :, :] * bim.transpose(0, 2, 1)[:, :, None, :] + cim[:, None, :, :] * bre.transpose(0, 2, 1)[:, :, None, :]
    cb = jnp.concatenate([cb_re, cb_im], axis=-1).reshape(g, SSM_GROUP * SSM_GROUP, 2 * p)
    bt_re, bt_im = bre.transpose(0, 2, 1), bim.transpose(0, 2, 1)
    bb1 = jnp.concatenate([bt_re, bt_im], axis=-1)
    bb2 = jnp.concatenate([-bt_im, bt_re], axis=-1)
    ct_re, ct_im = cre.transpose(0, 2, 1), cim.transpose(0, 2, 1)
    col_a = jnp.concatenate([ct_re, -ct_re], axis=1)
    col_b = jnp.concatenate([-ct_im, -ct_im], axis=1)
    ct = jnp.concatenate([col_a, col_b], axis=-1)
    dtab = jnp.broadcast_to(d.astype(F32).reshape(g, SSM_GROUP, 1), (g, SSM_GROUP, L))
    return lcr, lci, lrr, lri, cb, bb1, bb2, ct, dtab


def _state_to_lanes(re, im):
    b = re.shape[0]
    re = re.astype(F32).reshape(b, SSM_HALVES, HALF_RI)
    im = im.astype(F32).reshape(b, SSM_HALVES, HALF_RI)
    return jnp.concatenate([re, im], axis=-1).reshape(b, N_STATE)


def _lanes_to_state(h):
    b = h.shape[0]
    h = h.reshape(b, SSM_HALVES, 2, HALF_RI)
    return (h[:, :, 0].reshape(b, SSM_GROUPS, SSM_STATE), h[:, :, 1].reshape(b, SSM_GROUPS, SSM_STATE))


class _Plan(NamedTuple):
    nb: int
    tb: int
    chunk: int
    sub: int
    toeplitz: bool


def _plan(b, t):
    if t >= LAYER_ROWS:
        assert t % LAYER_ROWS == 0 and LAYER_ROWS % LAYER_CHUNK == 0 and LAYER_CHUNK % TOEP_CHUNK == 0
        return _Plan(nb=1, tb=LAYER_ROWS, chunk=LAYER_CHUNK, sub=LAYER_CHUNK, toeplitz=True)
    nb = min(b, LAYER_ROWS // t)
    assert b % nb == 0
    return _Plan(nb=nb, tb=t, chunk=t, sub=nb * t, toeplitz=False)


def _stream_step(x, pos0, ret_s0, h0_re, h0_im, weights):
    (norm_g, w_in, w_ut, ret_norm_g, lam_bar, b_bar, c_re, c_im, ssm_d, w_glu, b_glu, w_out,
     final_g) = weights
    b, t, _ = x.shape
    plan = _plan(b, t)
    cos, sin = _rotary_tables(pos0, t, plan.nb)
    dm, qd, sd, kd_rows = _retention_tables(plan.chunk, plan.sub)
    if not plan.toeplitz:
        ar, ai, bd, cd = _scan_tables(lam_bar, b_bar, c_re, c_im)
        out, s_fin, h_fin = _short_call(
            x, norm_g, w_in, cos, sin, kd_rows, _state_to_lanes(h0_re, h0_im), ar, ai, bd, cd,
            ssm_d[None, :].astype(F32), ret_s0, ret_norm_g, dm, qd, sd, w_glu, b_glu, w_out, final_g, plan)
        h_re, h_im = _lanes_to_state(h_fin)
        return out, s_fin, h_re, h_im
    ut, q, k, kdk, v = _front_call(x, norm_g, w_ut, w_in, cos, sin, kd_rows, plan.nb, plan.tb, plan.sub)
    h0 = jnp.concatenate([h0_re, h0_im], axis=-1).astype(F32).transpose(1, 0, 2)
    y, h_fin = _toeplitz_call(ut, h0, _toeplitz_tables(lam_bar, b_bar, c_re, c_im, ssm_d))
    h_fin = h_fin.transpose(1, 0, 2)
    h_re, h_im = h_fin[..., :SSM_STATE], h_fin[..., SSM_STATE:]
    out, s_fin = _layer_call(x, y, q, k, kdk, v, ret_s0, norm_g, w_in, ret_norm_g, dm, qd, sd,
                             w_glu, b_glu, w_out, final_g, plan)
    return out, s_fin, h_re, h_im


def kernel(x_prompt, x_sample, state_ret, state_ssm_re, state_ssm_im, norm_g, w_in, ret_norm_g,
           ssm_lambda_re, ssm_lambda_im, ssm_log_step, ssm_b_re, ssm_b_im, ssm_c_re, ssm_c_im,
           ssm_d, w_glu, b_glu, w_out, final_norm_g):
    depth = w_in.shape[0]
    assert depth == 1, "single-layer trunk"
    bp, tp, _ = x_prompt.shape
    bs, ts, _ = x_sample.shape
    l = 0
    w = w_in[l].astype(BF16)
    w_ut = w[:, W_IN_U_BLOCK * SSM_WIDTH:(W_IN_U_BLOCK + 1) * SSM_WIDTH].T
    lam_bar, b_bar = _discretise(ssm_lambda_re[l], ssm_lambda_im[l], ssm_log_step[l], ssm_b_re[l], ssm_b_im[l])
    weights = (
        norm_g[l][None, :].astype(F32), w, w_ut, ret_norm_g[l].reshape(1, RET_WIDTH).astype(F32),
        lam_bar, b_bar, ssm_c_re[l], ssm_c_im[l], ssm_d[l],
        w_glu[l].astype(BF16), b_glu[l][None, :].astype(F32),
        w_out[l].astype(BF16), final_norm_g[None, :].astype(F32),
    )

    ret0_p = jnp.zeros((bp, RET_HEADS, HEAD_DIM, HEAD_DIM), F32)
    h0_p = jnp.zeros((bp, SSM_GROUPS, SSM_STATE), F32)
    y_p, r_p, hp_re, hp_im = _stream_step(x_prompt, 0, ret0_p, h0_p, h0_p, weights)
    y_s, r_s, hs_re, hs_im = _stream_step(x_sample, PAST_LEN, state_ret[l].astype(F32),
                                          state_ssm_re[l], state_ssm_im[l], weights)
    return (y_p, y_s, r_p[None], hp_re[None], hp_im[None], r_s[None], hs_re[None], hs_im[None])
```

```python
import functools
from typing import NamedTuple

import jax
import jax.numpy as jnp
import numpy as np
from jax import lax
from jax.experimental import pallas as pl
from jax.experimental.pallas import tpu as pltpu

D_MODEL = 1024
RET_HEADS = 4
HEAD_DIM = 128
RET_WIDTH = RET_HEADS * HEAD_DIM
SSM_WIDTH = 512
SSM_GROUP = 16
SSM_GROUPS = 32
SSM_STATE = 64
ROPE_BASE = 10000.0
EPS = 1e-6
LAMBDA_RE_MAX = -1e-4
PAST_LEN = 2048

LAYER_ROWS = 1024
LAYER_CHUNK = 256
TOEP_CHUNK = 128
TOEP_LANES = SSM_GROUP * TOEP_CHUNK

W_IN_G_RET_BLOCK = 3
W_IN_U_BLOCK = 4
W_IN_G_SSM_BLOCK = 5

SSM_HALVES = 2
HALF_IN = SSM_WIDTH // SSM_HALVES
HALF_RI = (SSM_GROUPS // SSM_HALVES) * SSM_STATE
HALF_STATE = 2 * HALF_RI
N_STATE = SSM_HALVES * HALF_STATE
SCAN_LANES = 512

VMEM_LIMIT_BYTES = 56 * 1024 * 1024

F32 = jnp.float32
BF16 = jnp.bfloat16


def _rms(x, g):
    return x * lax.rsqrt(jnp.mean(x * x, axis=-1, keepdims=True) + EPS) * g


def _ssm_kernel(x_ref, g_ref, wu_ref, h0_ref, ar_ref, ai_ref, bd_ref, cd_ref, d_ref, y_ref, hfin_ref,
                ubt_ref, utb_ref, bu_ref, hs_ref, ytb_ref):
    nb, tb, d_model = x_ref.shape
    w = SSM_WIDTH
    rows = nb * tb

    @pl.when(pl.program_id(0) == 0)
    def _():
        hfin_ref[...] = h0_ref[...]

    hn = _rms(x_ref[...].reshape(rows, d_model), g_ref[...]).astype(BF16)
    ubt_ref[...] = jnp.dot(hn, wu_ref[...], preferred_element_type=F32).reshape(nb, tb, w)
    for t in range(tb):
        utb_ref[t] = ubt_ref[:, t, :]
    u = utb_ref[...].reshape(rows, w)
    ub = u.astype(BF16)

    for h in range(SSM_HALVES):
        cols = slice(h * HALF_IN, (h + 1) * HALF_IN)
        yh = d_ref[:, cols] * u[:, cols]
        for q in range(HALF_RI // SCAN_LANES):
            re_h = q * SCAN_LANES
            im_h = HALF_RI + re_h
            re = h * HALF_STATE + re_h
            im = h * HALF_STATE + im_h
            bu_ref[:, re:re + SCAN_LANES] = jnp.dot(
                ub[:, cols], bd_ref[h, :, re_h:re_h + SCAN_LANES], preferred_element_type=F32)
            bu_ref[:, im:im + SCAN_LANES] = jnp.dot(
                ub[:, cols], bd_ref[h, :, im_h:im_h + SCAN_LANES], preferred_element_type=F32)
            ar = jnp.broadcast_to(ar_ref[h:h + 1, re_h:re_h + SCAN_LANES], (nb, SCAN_LANES))
            ai = jnp.broadcast_to(ai_ref[h:h + 1, re_h:re_h + SCAN_LANES], (nb, SCAN_LANES))
            hr = hfin_ref[:, re:re + SCAN_LANES]
            hi = hfin_ref[:, im:im + SCAN_LANES]
            for t in range(tb):
                r = slice(t * nb, (t + 1) * nb)
                hr, hi = (ar * hr - ai * hi + bu_ref[r, re:re + SCAN_LANES],
                          ar * hi + ai * hr + bu_ref[r, im:im + SCAN_LANES])
                hs_ref[r, re:re + SCAN_LANES] = hr.astype(BF16)
                hs_ref[r, im:im + SCAN_LANES] = hi.astype(BF16)
            hfin_ref[:, re:re + SCAN_LANES] = hr
            hfin_ref[:, im:im + SCAN_LANES] = hi
            yh = yh + jnp.dot(hs_ref[:, re:re + SCAN_LANES], cd_ref[h, re_h:re_h + SCAN_LANES, :],
                              preferred_element_type=F32)
            yh = yh + jnp.dot(hs_ref[:, im:im + SCAN_LANES], cd_ref[h, im_h:im_h + SCAN_LANES, :],
                              preferred_element_type=F32)
        ytb_ref[:, :, cols] = yh.reshape(tb, nb, HALF_IN)
    for b in range(nb):
        y_ref[b] = ytb_ref[:, b, :]


def _front_kernel(emit_ut, sub, x_ref, g_ref, *refs):
    if emit_ut:
        wu_ref, wqkv_ref, cos_ref, sin_ref, kd_ref, ut_ref, q_ref, k_ref, kdk_ref, v_ref = refs
    else:
        wqkv_ref, cos_ref, sin_ref, kd_ref, q_ref, k_ref, kdk_ref, v_ref = refs
    nb, tb, d = x_ref.shape
    rows = nb * tb

    def load_rows(ref, r0):
        if nb == 1:
            return ref[0, r0:r0 + sub, :]
        return ref[r0 // tb:(r0 + sub) // tb].reshape(sub, ref.shape[-1])

    def store_rows(ref, r0, lanes, val):
        if nb == 1:
            ref[0, r0:r0 + sub, lanes] = val
        else:
            ref[r0 // tb:(r0 + sub) // tb, :, lanes] = val.reshape(sub // tb, tb, val.shape[-1])

    us = []
    for r0 in range(0, rows, sub):
        hn = _rms(load_rows(x_ref, r0), g_ref[...]).astype(BF16)
        if emit_ut:
            us.append(jnp.dot(hn, wu_ref[...], preferred_element_type=F32))
        qkv = jnp.dot(hn, wqkv_ref[...], preferred_element_type=F32)
        cos = cos_ref[r0:r0 + sub, :]
        sin = sin_ref[r0:r0 + sub, :]

        def rotary(a):
            return a * cos + pltpu.roll(a, HEAD_DIM // 2, 1) * sin

        for h in range(RET_HEADS):
            lanes = slice(h * HEAD_DIM, (h + 1) * HEAD_DIM)
            k = rotary(qkv[:, RET_WIDTH + h * HEAD_DIM:RET_WIDTH + (h + 1) * HEAD_DIM]) * (HEAD_DIM ** -0.5)
            store_rows(q_ref, r0, lanes, rotary(qkv[:, lanes]).astype(BF16))
            store_rows(k_ref, r0, lanes, k.astype(BF16))
            store_rows(kdk_ref, r0, lanes, (k * kd_ref[:, lanes]).astype(BF16))
            store_rows(v_ref, r0, lanes,
                       qkv[:, 2 * RET_WIDTH + h * HEAD_DIM:2 * RET_WIDTH + (h + 1) * HEAD_DIM].astype(BF16))

    if emit_ut:
        nk = rows // TOEP_CHUNK
        ut = jnp.concatenate(us, axis=0).T
        for g in range(SSM_GROUPS):
            grp = slice(g * SSM_GROUP, (g + 1) * SSM_GROUP)
            slabs = jnp.stack([ut[grp, kk * TOEP_CHUNK:(kk + 1) * TOEP_CHUNK] for kk in range(nk)])
            by_channel = jnp.swapaxes(slabs, 0, 1)
            for c in range(SSM_GROUP):
                ut_ref[g, :, c * TOEP_CHUNK:(c + 1) * TOEP_CHUNK] = by_channel[c]


def _quarters(ref):
    return [ref.at[:, :, pl.ds(i * RET_WIDTH, RET_WIDTH)] for i in range(4)]


def _front_call(x, norm_g, w_in, cos, sin, kd_rows, nb, tb, sub):
    b, t, d = x.shape
    rows = nb * tb
    c2 = lambda i, j: (0, 0)

    def body(*refs):
        _front_kernel(True, sub, *refs[:-1], *_quarters(refs[-1]))

    return pl.pallas_call(
        body,
        grid=(b // nb, t // tb),
        in_specs=[
            pl.BlockSpec((nb, tb, d), lambda i, j: (i, j, 0)),
            pl.BlockSpec((1, d), c2),
            pl.BlockSpec((d, SSM_WIDTH), lambda i, j: (0, W_IN_U_BLOCK)),
            pl.BlockSpec((d, 3 * RET_WIDTH), c2),
            pl.BlockSpec((rows, HEAD_DIM), lambda i, j: (j, 0)),
            pl.BlockSpec((rows, HEAD_DIM), lambda i, j: (j, 0)),
            pl.BlockSpec((sub, RET_WIDTH), c2),
        ],
        out_specs=[pl.BlockSpec((SSM_GROUPS, None, tb // TOEP_CHUNK, TOEP_LANES), lambda i, j: (0, i, j, 0)),
                   pl.BlockSpec((nb, tb, 4 * RET_WIDTH), lambda i, j: (i, j, 0))],
        out_shape=[jax.ShapeDtypeStruct((SSM_GROUPS, b, t // TOEP_CHUNK, TOEP_LANES), F32),
                   jax.ShapeDtypeStruct((b, t, 4 * RET_WIDTH), BF16)],
        compiler_params=pltpu.CompilerParams(
            dimension_semantics=("arbitrary", "arbitrary"), vmem_limit_bytes=VMEM_LIMIT_BYTES),
        name="qkv_rotary_front",
    )(x, norm_g, w_in, w_in, cos, sin, kd_rows)


def _powers(base_re, base_im, exponent, n_bits):
    pr = jnp.ones_like(base_re)
    pi = jnp.zeros_like(base_re)
    br, bi = base_re, base_im
    for bit in range(n_bits):
        take = ((exponent >> bit) & 1) == 1
        pr, pi = jnp.where(take, pr * br - pi * bi, pr), jnp.where(take, pr * bi + pi * br, pi)
        br, bi = br * br - bi * bi, 2.0 * br * bi
    return pr, pi


def _split_bf16(x):
    hi = x.astype(BF16)
    r = x - hi.astype(F32)
    mid = r.astype(BF16)
    lo = (r - mid.astype(F32)).astype(BF16)
    return hi, mid, lo


def _dot_f32(a, b):
    a3, b3 = _split_bf16(a), _split_bf16(b)
    acc = None
    for i in range(3):
        for j in range(3 - i):
            term = jnp.dot(a3[i], b3[j], preferred_element_type=F32)
            acc = term if acc is None else acc + term
    return acc


def _toeplitz_kernel(u_ref, h0_ref, lcr_ref, lci_ref, lrr_ref, lri_ref, cb_ref, bb1_ref, bb2_ref,
                     ct_ref, d_ref, y_ref, hfin_ref, a_ref, wb_ref, wc_ref, t_ref, ys_ref, yi_ref):
    nb, nk, _ = u_ref.shape
    rows = nb * nk
    L = TOEP_CHUNK
    n_bits = L.bit_length() - 1
    sub_i = lax.broadcasted_iota(jnp.int32, (L, L), 0)
    lane_i = lax.broadcasted_iota(jnp.int32, (L, L), 1)

    lcr, lci = lcr_ref[...], lci_ref[...]
    lane_p = lax.broadcasted_iota(jnp.int32, lcr.shape, 1)
    p0r, p0i = _powers(lcr, lci, lane_p, n_bits)
    p1r, p1i = p0r * lcr - p0i * lci, p0r * lci + p0i * lcr

    taps = _dot_f32(cb_ref[...], jnp.concatenate([p0r, -p0i], axis=0))

    lrr = jnp.broadcast_to(lrr_ref[...], (L, L))
    lri = jnp.broadcast_to(lri_ref[...], (L, L))
    qr, qi = _powers(lrr, lri, (L - 1) - sub_i, n_bits)
    for c in range(SSM_GROUP):
        wb_ref[c * L:(c + 1) * L, :] = (qr * bb1_ref[c:c + 1, :] + qi * bb2_ref[c:c + 1, :]).astype(BF16)
    pa = jnp.concatenate([p1r, p1i], axis=0)
    pb = jnp.concatenate([p1i, p1r], axis=0)
    for c in range(SSM_GROUP):
        wc_ref[:, c * L:(c + 1) * L] = (
            jnp.broadcast_to(ct_ref[:, c:c + 1], (L, L)) * pa
            + jnp.broadcast_to(ct_ref[:, SSM_GROUP + c:SSM_GROUP + c + 1], (L, L)) * pb).astype(BF16)

    a_ref[...] = u_ref[...].reshape(rows, TOEP_LANES).astype(BF16)

    s_in = jnp.dot(a_ref[...], wb_ref[...], preferred_element_type=F32)
    s_in = jnp.swapaxes(s_in.reshape(nb, nk, L), 0, 1)
    l128r = qr[0:1] * lrr[0:1] - qi[0:1] * lri[0:1]
    l128i = qr[0:1] * lri[0:1] + qi[0:1] * lrr[0:1]
    half = L // 2
    is_re = lane_i[0:1] < half
    mul_same = jnp.broadcast_to(l128r, (nb, L))
    mul_swap = jnp.broadcast_to(jnp.where(is_re, -l128i, l128i), (nb, L))
    s_sw = pltpu.roll(s_in.reshape(nk * nb, L), half, 1).reshape(nk, nb, L)
    h = h0_ref[...]
    h_sw = pltpu.roll(h, half, 1)
    h_prev = []
    for k in range(nk):
        h_prev.append(h)
        h, h_sw = (h * mul_same + h_sw * mul_swap + s_in[k],
                   h_sw * mul_same - h * mul_swap + s_sw[k])
    hfin_ref[...] = h
    h_prev = jnp.swapaxes(jnp.stack(h_prev), 0, 1).reshape(rows, L).astype(BF16)

    causal = lane_i >= sub_i
    for pair in range(SSM_GROUP // 2):
        slot = pair
        for ci in range(SSM_GROUP):
            for cc in range(2):
                r = ci * SSM_GROUP + 2 * pair + cc
                tap = jnp.broadcast_to(taps[r:r + 1, :], (L, L))
                toep = jnp.where(causal, pltpu.roll(tap, 0, 1, stride=1, stride_axis=0), 0.0)
                t_ref[slot, ci * L:(ci + 1) * L, cc * L:(cc + 1) * L] = toep.astype(BF16)
        yv = jnp.dot(a_ref[...], t_ref[slot], preferred_element_type=F32)
        for cc in range(2):
            c = 2 * pair + cc
            uc = u_ref[:, :, c * L:(c + 1) * L]
            ys_ref[c] = yv[:, cc * L:(cc + 1) * L].reshape(nb, nk, L) + d_ref[c:c + 1, :] * uc
    inter = jnp.dot(h_prev, wc_ref[...], preferred_element_type=F32)
    for c in range(SSM_GROUP):
        yi_ref[c] = inter[:, c * L:(c + 1) * L].reshape(nb, nk, L)
    for b in range(nb):
        y_ref[b] = jnp.swapaxes(ys_ref[:, b] + yi_ref[:, b], 0, 1)


def _toeplitz_call(ut, h0, tabs):
    g, nb, nk, lanes = ut.shape
    L = TOEP_CHUNK
    rows = nb * nk
    lcr, lci, lrr, lri, cb, bb1, bb2, ct, dtab = tabs

    def per_group(*shape):
        return pl.BlockSpec((None,) + shape, lambda i: (i,) + (0,) * len(shape))

    return pl.pallas_call(
        _toeplitz_kernel,
        grid=(g,),
        in_specs=[
            per_group(nb, nk, lanes), per_group(nb, L),
            per_group(SSM_STATE, L), per_group(SSM_STATE, L), per_group(1, L), per_group(1, L),
            per_group(SSM_GROUP * SSM_GROUP, L), per_group(SSM_GROUP, L), per_group(SSM_GROUP, L),
            per_group(L, 2 * SSM_GROUP), per_group(SSM_GROUP, L),
        ],
        out_specs=[per_group(nb, nk, SSM_GROUP, L), per_group(nb, L)],
        out_shape=[
            jax.ShapeDtypeStruct((g, nb, nk, SSM_GROUP, L), F32),
            jax.ShapeDtypeStruct((g, nb, L), F32),
        ],
        scratch_shapes=[
            pltpu.VMEM((rows, lanes), BF16),
            pltpu.VMEM((lanes, L), BF16),
            pltpu.VMEM((L, lanes), BF16),
            pltpu.VMEM((SSM_GROUP // 2, lanes, 2 * L), BF16),
            pltpu.VMEM((SSM_GROUP, nb, nk, L), F32),
            pltpu.VMEM((SSM_GROUP, nb, nk, L), F32),
        ],
        compiler_params=pltpu.CompilerParams(
            dimension_semantics=("arbitrary",), vmem_limit_bytes=VMEM_LIMIT_BYTES),
        name="ssm_toeplitz",
    )(ut, h0, lcr, lci, lrr, lri, cb, bb1, bb2, ct, dtab)


def _layer_kernel(chunk, sub, y_by_group, x_ref, y_ref, q_ref, k_ref, kdk_ref, v_ref, s0_ref, ng_ref,
                  wgr_ref, wgs_ref, rg_ref, dm_ref, qd_ref, sd_ref, wglu_ref, bglu_ref, wo_ref, fg_ref,
                  out_ref, sfin_ref, o_ref):
    nb, tb, d = x_ref.shape
    rows = nb * tb

    @pl.when(pl.program_id(1) == 0)
    def _():
        sfin_ref[...] = s0_ref[...]

    def load_rows(ref, r0, lanes=slice(None)):
        if nb == 1:
            return ref[0, r0:r0 + sub, lanes]
        val = ref[r0 // tb:(r0 + sub) // tb, :, lanes]
        return val.reshape(sub, val.shape[-1])

    def store_rows(ref, r0, val):
        if nb == 1:
            ref[0, r0:r0 + sub, :] = val
        else:
            ref[r0 // tb:(r0 + sub) // tb] = val.reshape(sub // tb, tb, ref.shape[-1])

    rg = rg_ref[...]
    for r0 in range(0, rows, sub):
        x = load_rows(x_ref, r0)
        hn = _rms(x, ng_ref[...]).astype(BF16)
        g_ret = jnp.dot(hn, wgr_ref[...], preferred_element_type=F32)
        g_ssm = jnp.dot(hn, wgs_ref[...], preferred_element_type=F32)

        for h in range(RET_HEADS):
            lanes = slice(h * HEAD_DIM, (h + 1) * HEAD_DIM)
            q, kb, kdk, vb = (load_rows(ref, r0, lanes) for ref in (q_ref, k_ref, kdk_ref, v_ref))
            dm = dm_ref[h]
            qd = qd_ref[h]
            sd = sd_ref[h]
            for c0 in range(0, sub, chunk):
                n = (r0 + c0) // tb
                r = slice(c0, c0 + chunk)
                qc, kc, vc = q[r], kb[r], vb[r]
                s = sfin_ref[n, h]
                scores = lax.dot_general(qc, kc, (((1,), (1,)), ((), ())), preferred_element_type=F32)
                p = (scores * dm).astype(BF16)
                o = jnp.dot(p, vc, preferred_element_type=F32)
                o = o + jnp.dot(qc, s.astype(BF16), preferred_element_type=F32) * qd
                sfin_ref[n, h] = sd * s + jnp.dot(kdk[r].T, vc, preferred_element_type=F32)
                o_ref[r0 + c0:r0 + c0 + chunk, lanes] = o

        o_parts = []
        for h in range(RET_HEADS):
            lanes = slice(h * HEAD_DIM, (h + 1) * HEAD_DIM)
            o_parts.append(_rms(o_ref[r0:r0 + sub, lanes], rg[:, lanes]))
        o = jnp.concatenate(o_parts, axis=-1) * jax.nn.silu(g_ret)

        if y_by_group:
            k0 = r0 // TOEP_CHUNK
            y = jnp.concatenate(
                [y_ref[:, k0 + j].reshape(SSM_WIDTH, TOEP_CHUNK).T for j in range(sub // TOEP_CHUNK)], axis=0)
        else:
            y = load_rows(y_ref, r0)
        z = jax.nn.gelu(y)
        gate = jnp.dot(z.astype(BF16), wglu_ref[...], preferred_element_type=F32) + bglu_ref[...]
        z = z * jax.nn.sigmoid(gate)
        z = z * jax.nn.silu(g_ssm)

        mix = jnp.concatenate([o, z], axis=-1).astype(BF16)
        res = x + jnp.dot(mix, wo_ref[...], preferred_element_type=F32)
        store_rows(out_ref, r0, _rms(res, fg_ref[...]))


def _layer_call(x, y, qkv, s0, norm_g, w_in, ret_norm_g, dm, qd, sd, w_glu, b_glu,
                w_out, final_g, plan):
    b, t, d = x.shape
    nb, tb, chunk, sub = plan.nb, plan.tb, plan.chunk, plan.sub
    rows = nb * tb
    c2 = lambda i, j: (0, 0)
    c3 = lambda i, j: (0, 0, 0)
    state_spec = pl.BlockSpec((nb, RET_HEADS, HEAD_DIM, HEAD_DIM), lambda i, j: (i, 0, 0, 0))
    row_spec = pl.BlockSpec((nb, tb, RET_WIDTH), lambda i, j: (i, j, 0))
    y_spec = pl.BlockSpec((SSM_GROUPS, None, tb // TOEP_CHUNK, SSM_GROUP, TOEP_CHUNK),
                          lambda i, j: (0, i, j, 0, 0))
    def body(x_ref, y_ref, qkv_ref, *refs):
        _layer_kernel(chunk, sub, True, x_ref, y_ref, *_quarters(qkv_ref), *refs)

    return pl.pallas_call(
        body,
        grid=(b // nb, t // tb),
        in_specs=[
            pl.BlockSpec((nb, tb, d), lambda i, j: (i, j, 0)),
            y_spec,
            pl.BlockSpec((nb, tb, 4 * RET_WIDTH), lambda i, j: (i, j, 0)),
            state_spec,
            pl.BlockSpec((1, d), c2),
            pl.BlockSpec((d, RET_WIDTH), lambda i, j: (0, W_IN_G_RET_BLOCK)),
            pl.BlockSpec((d, SSM_WIDTH), lambda i, j: (0, W_IN_G_SSM_BLOCK)),
            pl.BlockSpec((1, RET_WIDTH), c2),
            pl.BlockSpec((RET_HEADS, chunk, chunk), c3),
            pl.BlockSpec((RET_HEADS, chunk, HEAD_DIM), c3),
            pl.BlockSpec((RET_HEADS, HEAD_DIM, HEAD_DIM), c3),
            pl.BlockSpec((SSM_WIDTH, SSM_WIDTH), c2),
            pl.BlockSpec((1, SSM_WIDTH), c2),
            pl.BlockSpec((D_MODEL, D_MODEL), c2),
            pl.BlockSpec((1, d), c2),
        ],
        out_specs=[
            pl.BlockSpec((nb, tb, d), lambda i, j: (i, j, 0)),
            state_spec,
        ],
        out_shape=[
            jax.ShapeDtypeStruct((b, t, d), F32),
            jax.ShapeDtypeStruct((b, RET_HEADS, HEAD_DIM, HEAD_DIM), F32),
        ],
        scratch_shapes=[pltpu.VMEM((rows, RET_WIDTH), F32)],
        compiler_params=pltpu.CompilerParams(
            dimension_semantics=("arbitrary", "arbitrary"), vmem_limit_bytes=VMEM_LIMIT_BYTES),
        name="retention_glu_layer",
    )(x, y, qkv, s0, norm_g, w_in, w_in, ret_norm_g, dm, qd, sd, w_glu, b_glu, w_out, final_g)


def _short_kernel(chunk, sub, x_ref, ng_ref, wqkv_ref, cos_ref, sin_ref, kd_ref, wu_ref, h0_ref, ar_ref,
                  ai_ref, bd_ref, cd_ref, d_ref, s0_ref, wgr_ref, wgs_ref, rg_ref, dm_ref, qd_ref, sd_ref,
                  wglu_ref, bglu_ref, wo_ref, fg_ref, out_ref, sfin_ref, hfin_ref,
                  q_ref, k_ref, kdk_ref, v_ref, y_ref, ubt_ref, utb_ref, bu_ref, hs_ref, ytb_ref, o_ref):
    _front_kernel(False, sub, x_ref, ng_ref, wqkv_ref, cos_ref, sin_ref, kd_ref, q_ref, k_ref, kdk_ref, v_ref)
    _ssm_kernel(x_ref, ng_ref, wu_ref, h0_ref, ar_ref, ai_ref, bd_ref, cd_ref, d_ref, y_ref, hfin_ref,
                ubt_ref, utb_ref, bu_ref, hs_ref, ytb_ref)
    _layer_kernel(chunk, sub, False, x_ref, y_ref, q_ref, k_ref, kdk_ref, v_ref, s0_ref, ng_ref, wgr_ref,
                  wgs_ref, rg_ref, dm_ref, qd_ref, sd_ref, wglu_ref, bglu_ref, wo_ref, fg_ref,
                  out_ref, sfin_ref, o_ref)


def _short_call(x, norm_g, w_in, cos, sin, kd_rows, h0, ar, ai, bd, cd, d, s0, ret_norm_g, dm, qd, sd,
                w_glu, b_glu, w_out, final_g, plan):
    b, t, dm_ = x.shape
    chunk, sub = plan.chunk, plan.sub
    assert plan.nb == b and plan.tb == t, "short streams are handled as one block"
    rows = b * t
    c2 = lambda i, j: (0, 0)
    c3 = lambda i, j: (0, 0, 0)
    c4 = lambda i, j: (0, 0, 0, 0)
    full3 = lambda w: pl.BlockSpec((b, t, w), c3)
    state_spec = pl.BlockSpec((b, RET_HEADS, HEAD_DIM, HEAD_DIM), c4)
    return pl.pallas_call(
        functools.partial(_short_kernel, chunk, sub),
        grid=(1, 1),
        in_specs=[
            full3(dm_),
            pl.BlockSpec((1, dm_), c2),
            pl.BlockSpec((dm_, 3 * RET_WIDTH), c2),
            pl.BlockSpec((rows, HEAD_DIM), c2),
            pl.BlockSpec((rows, HEAD_DIM), c2),
            pl.BlockSpec((sub, RET_WIDTH), c2),
            pl.BlockSpec((dm_, SSM_WIDTH), lambda i, j: (0, W_IN_U_BLOCK)),
            pl.BlockSpec((b, N_STATE), c2),
            pl.BlockSpec((SSM_HALVES, HALF_RI), c2),
            pl.BlockSpec((SSM_HALVES, HALF_RI), c2),
            pl.BlockSpec((SSM_HALVES, HALF_IN, HALF_STATE), c3),
            pl.BlockSpec((SSM_HALVES, HALF_STATE, HALF_IN), c3),
            pl.BlockSpec((1, SSM_WIDTH), c2),
            state_spec,
            pl.BlockSpec((dm_, RET_WIDTH), lambda i, j: (0, W_IN_G_RET_BLOCK)),
            pl.BlockSpec((dm_, SSM_WIDTH), lambda i, j: (0, W_IN_G_SSM_BLOCK)),
            pl.BlockSpec((1, RET_WIDTH), c2),
            pl.BlockSpec((RET_HEADS, chunk, chunk), c3),
            pl.BlockSpec((RET_HEADS, chunk, HEAD_DIM), c3),
            pl.BlockSpec((RET_HEADS, HEAD_DIM, HEAD_DIM), c3),
            pl.BlockSpec((SSM_WIDTH, SSM_WIDTH), c2),
            pl.BlockSpec((1, SSM_WIDTH), c2),
            pl.BlockSpec((D_MODEL, D_MODEL), c2),
            pl.BlockSpec((1, dm_), c2),
        ],
        out_specs=[full3(dm_), state_spec, pl.BlockSpec((b, N_STATE), c2)],
        out_shape=[
            jax.ShapeDtypeStruct((b, t, dm_), F32),
            jax.ShapeDtypeStruct((b, RET_HEADS, HEAD_DIM, HEAD_DIM), F32),
            jax.ShapeDtypeStruct((b, N_STATE), F32),
        ],
        scratch_shapes=[pltpu.VMEM((b, t, RET_WIDTH), BF16)] * 4 + [
            pltpu.VMEM((b, t, SSM_WIDTH), F32),
            pltpu.VMEM((b, t, SSM_WIDTH), F32),
            pltpu.VMEM((t, b, SSM_WIDTH), F32),
            pltpu.VMEM((rows, N_STATE), F32),
            pltpu.VMEM((rows, N_STATE), BF16),
            pltpu.VMEM((t, b, SSM_WIDTH), F32),
            pltpu.VMEM((rows, RET_WIDTH), F32),
        ],
        compiler_params=pltpu.CompilerParams(
            dimension_semantics=("arbitrary", "arbitrary"), vmem_limit_bytes=VMEM_LIMIT_BYTES),
        name="short_stream_step",
    )(x, norm_g, w_in, cos, sin, kd_rows, w_in, h0, ar, ai, bd, cd, d, s0, w_in, w_in, ret_norm_g, dm, qd, sd,
      w_glu, b_glu, w_out, final_g)


def _rotary_tables(pos0, steps, reps):
    f = np.float64
    half = HEAD_DIM // 2
    inv = f(ROPE_BASE) ** (-np.arange(half, dtype=f) / f(half))
    ang = (f(pos0) + np.arange(steps, dtype=f))[:, None] * inv[None, :]
    cos, sin = np.cos(ang), np.sin(ang)
    cos2 = np.concatenate([cos, cos], axis=-1)
    sin2 = np.concatenate([-sin, sin], axis=-1)
    return jnp.asarray(np.tile(cos2, (reps, 1)), F32), jnp.asarray(np.tile(sin2, (reps, 1)), F32)


def _retention_tables(chunk, sub):
    f = np.float64
    log_g = np.log1p(-np.power(f(2.0), f(-5.0) - np.arange(RET_HEADS, dtype=f)))
    idx = np.arange(chunk, dtype=f)
    diff = idx[:, None] - idx[None, :]
    causal = diff >= 0
    dm = np.where(causal[None], np.exp(np.where(causal, diff, f(0.0))[None] * log_g[:, None, None]), f(0.0))
    q_decay = np.exp((idx + f(1.0))[None, :] * log_g[:, None])
    k_decay = np.exp((f(chunk) - f(1.0) - idx)[None, :] * log_g[:, None])
    s_decay = np.exp(f(chunk) * log_g)
    qd = np.broadcast_to(q_decay[:, :, None], (RET_HEADS, chunk, HEAD_DIM))
    sd = np.broadcast_to(s_decay[:, None, None], (RET_HEADS, HEAD_DIM, HEAD_DIM))
    kd = np.broadcast_to(k_decay.T[:, :, None], (chunk, RET_HEADS, HEAD_DIM)).reshape(chunk, RET_WIDTH)
    kd_rows = np.tile(kd, (sub // chunk, 1))
    return tuple(jnp.asarray(np.ascontiguousarray(a), F32) for a in (dm, qd, sd, kd_rows))


def _discretise(lambda_re, lambda_im, log_step, b_re, b_im):
    a = jnp.minimum(lambda_re.astype(F32), LAMBDA_RE_MAX)
    b = lambda_im.astype(F32)
    dt = jnp.exp(log_step.astype(F32))[:, None]
    mag = jnp.exp(a * dt)
    lre, lim = mag * jnp.cos(b * dt), mag * jnp.sin(b * dt)
    den = a * a + b * b
    cr = ((lre - 1.0) * a + lim * b) / den
    ci = (lim * a - (lre - 1.0) * b) / den
    b_re, b_im = b_re.astype(F32), b_im.astype(F32)
    bre = cr[..., None] * b_re - ci[..., None] * b_im
    bim = cr[..., None] * b_im + ci[..., None] * b_re
    return (lre, lim), (bre, bim)


def _scan_tables(lam_bar, b_bar, c_re, c_im):
    gl = SSM_GROUPS // SSM_HALVES
    eye = jnp.eye(gl, dtype=F32)
    ar = lam_bar[0].reshape(SSM_HALVES, HALF_RI)
    ai = lam_bar[1].reshape(SSM_HALVES, HALF_RI)

    def place_b(m):
        m = m.reshape(SSM_HALVES, gl, SSM_STATE, SSM_GROUP)
        return jnp.einsum("hgpc,gk->hgckp", m, eye).reshape(SSM_HALVES, HALF_IN, HALF_RI)

    def place_c(m):
        m = m.reshape(SSM_HALVES, gl, SSM_GROUP, SSM_STATE)
        return jnp.einsum("hgcp,gk->hkpgc", m, eye).reshape(SSM_HALVES, HALF_RI, HALF_IN)

    bd = jnp.concatenate([place_b(b_bar[0]), place_b(b_bar[1])], axis=-1).astype(BF16)
    cd = jnp.concatenate([place_c(c_re.astype(F32)), place_c(-c_im.astype(F32))], axis=1).astype(BF16)
    return ar, ai, bd, cd


def _toeplitz_tables(lam_bar, b_bar, c_re, c_im, d):
    g, p = lam_bar[0].shape
    L = TOEP_CHUNK
    lre, lim = lam_bar
    bre, bim = b_bar
    cre, cim = c_re.astype(F32), c_im.astype(F32)
    lcr = jnp.broadcast_to(lre[:, :, None], (g, p, L))
    lci = jnp.broadcast_to(lim[:, :, None], (g, p, L))
    lrr = jnp.concatenate([lre, lre], axis=-1)[:, None, :]
    lri = jnp.concatenate([lim, lim], axis=-1)[:, None, :]
    cb_re = cre[:, None, :, :] * bre.transpose(0, 2, 1)[:, :, None, :] - cim[:, None, :, :] * bim.transpose(0, 2, 1)[:, :, None, :]
    cb_im = cre[:, None, :, :] * bim.transpose(0, 2, 1)[:, :, None, :] + cim[:, None, :, :] * bre.transpose(0, 2, 1)[:, :, None, :]
    cb = jnp.concatenate([cb_re, cb_im], axis=-1).reshape(g, SSM_GROUP * SSM_GROUP, 2 * p)
    bt_re, bt_im = bre.transpose(0, 2, 1), bim.transpose(0, 2, 1)
    bb1 = jnp.concatenate([bt_re, bt_im], axis=-1)
    bb2 = jnp.concatenate([-bt_im, bt_re], axis=-1)
    ct_re, ct_im = cre.transpose(0, 2, 1), cim.transpose(0, 2, 1)
    col_a = jnp.concatenate([ct_re, -ct_re], axis=1)
    col_b = jnp.concatenate([-ct_im, -ct_im], axis=1)
    ct = jnp.concatenate([col_a, col_b], axis=-1)
    dtab = jnp.broadcast_to(d.astype(F32).reshape(g, SSM_GROUP, 1), (g, SSM_GROUP, L))
    return lcr, lci, lrr, lri, cb, bb1, bb2, ct, dtab


def _state_to_lanes(re, im):
    b = re.shape[0]
    re = re.astype(F32).reshape(b, SSM_HALVES, HALF_RI)
    im = im.astype(F32).reshape(b, SSM_HALVES, HALF_RI)
    return jnp.concatenate([re, im], axis=-1).reshape(b, N_STATE)


def _lanes_to_state(h):
    b = h.shape[0]
    h = h.reshape(b, SSM_HALVES, 2, HALF_RI)
    return (h[:, :, 0].reshape(b, SSM_GROUPS, SSM_STATE), h[:, :, 1].reshape(b, SSM_GROUPS, SSM_STATE))


class _Plan(NamedTuple):
    nb: int
    tb: int
    chunk: int
    sub: int
    toeplitz: bool


def _plan(b, t):
    if t >= LAYER_ROWS:
        assert t % LAYER_ROWS == 0 and LAYER_ROWS % LAYER_CHUNK == 0 and LAYER_CHUNK % TOEP_CHUNK == 0
        return _Plan(nb=1, tb=LAYER_ROWS, chunk=LAYER_CHUNK, sub=LAYER_CHUNK, toeplitz=True)
    nb = min(b, LAYER_ROWS // t)
    assert b % nb == 0
    return _Plan(nb=nb, tb=t, chunk=t, sub=nb * t, toeplitz=False)


def _stream_step(x, pos0, ret_s0, h0_re, h0_im, weights):
    (norm_g, w_in, ret_norm_g, lam_bar, b_bar, c_re, c_im, ssm_d, w_glu, b_glu, w_out,
     final_g) = weights
    b, t, _ = x.shape
    plan = _plan(b, t)
    cos, sin = _rotary_tables(pos0, t, plan.nb)
    dm, qd, sd, kd_rows = _retention_tables(plan.chunk, plan.sub)
    if not plan.toeplitz:
        ar, ai, bd, cd = _scan_tables(lam_bar, b_bar, c_re, c_im)
        out, s_fin, h_fin = _short_call(
            x, norm_g, w_in, cos, sin, kd_rows, _state_to_lanes(h0_re, h0_im), ar, ai, bd, cd,
            ssm_d[None, :].astype(F32), ret_s0, ret_norm_g, dm, qd, sd, w_glu, b_glu, w_out, final_g, plan)
        h_re, h_im = _lanes_to_state(h_fin)
        return out, s_fin, h_re, h_im
    ut, qkv = _front_call(x, norm_g, w_in, cos, sin, kd_rows, plan.nb, plan.tb, plan.sub)
    h0 = jnp.concatenate([h0_re, h0_im], axis=-1).astype(F32).transpose(1, 0, 2)
    y, h_fin = _toeplitz_call(ut, h0, _toeplitz_tables(lam_bar, b_bar, c_re, c_im, ssm_d))
    h_fin = h_fin.transpose(1, 0, 2)
    h_re, h_im = h_fin[..., :SSM_STATE], h_fin[..., SSM_STATE:]
    out, s_fin = _layer_call(x, y, qkv, ret_s0, norm_g, w_in, ret_norm_g, dm, qd, sd,
                             w_glu, b_glu, w_out, final_g, plan)
    return out, s_fin, h_re, h_im


def kernel(x_prompt, x_sample, state_ret, state_ssm_re, state_ssm_im, norm_g, w_in, ret_norm_g,
           ssm_lambda_re, ssm_lambda_im, ssm_log_step, ssm_b_re, ssm_b_im, ssm_c_re, ssm_c_im,
           ssm_d, w_glu, b_glu, w_out, final_norm_g):
    depth = w_in.shape[0]
    assert depth == 1, "single-layer trunk"
    bp, tp, _ = x_prompt.shape
    bs, ts, _ = x_sample.shape
    l = 0
    w = w_in[l].astype(BF16)
    lam_bar, b_bar = _discretise(ssm_lambda_re[l], ssm_lambda_im[l], ssm_log_step[l], ssm_b_re[l], ssm_b_im[l])
    weights = (
        norm_g[l][None, :].astype(F32), w, ret_norm_g[l].reshape(1, RET_WIDTH).astype(F32),
        lam_bar, b_bar, ssm_c_re[l], ssm_c_im[l], ssm_d[l],
        w_glu[l].astype(BF16), b_glu[l][None, :].astype(F32),
        w_out[l].astype(BF16), final_norm_g[None, :].astype(F32),
    )

    ret0_p = jnp.zeros((bp, RET_HEADS, HEAD_DIM, HEAD_DIM), F32)
    h0_p = jnp.zeros((bp, SSM_GROUPS, SSM_STATE), F32)
    y_p, r_p, hp_re, hp_im = _stream_step(x_prompt, 0, ret0_p, h0_p, h0_p, weights)
    y_s, r_s, hs_re, hs_im = _stream_step(x_sample, PAST_LEN, state_ret[l].astype(F32),
                                          state_ssm_re[l], state_ssm_im[l], weights)
    return (y_p, y_s, r_p[None], hp_re[None], hp_im[None], r_s[None], hs_re[None], hs_im[None])
```
